```python
import jax, jax.numpy as jnp
from jax import lax
import numpy as np

D_MODEL = 1024
BATCH = 16
SEQ = 2048
DEPTH = 2
DEC_BATCH = 128
DEC_SEQ = 1
PAST_LEN = 16384
PAGE_SIZE = 128

HD = 64
DSA_HEADS = 8
IDX_HEADS = 8
IDX_DIM = 64
DSA_TOPK = 256
NSA_HEADS = 8
CMP_BLK = 32
CMP_HID = 2 * HD
SEL_BLK = 64
N_SEL = 16
WINDOW = 512
MLA_HEADS = 8
D_CQ = 256
D_C = 128
D_NOPE = 64
D_ROPE = 32
D_V = 64
D_FF = -(-(8 * D_MODEL) // (3 * 256)) * 256
ROPE_THETA = 10000.0
EPS = 1e-6
Q_BLK = 128
G_BLK = 64
NEG = -1e30
FORCE = 1e9

DSA_ROW = 2 * HD + IDX_DIM
NSA_ROW = 4 * HD
MLA_ROW = D_C + D_ROPE
WIN_ROW = 2 * HD

IN_SPLITS = (
    ('dsa_q', DSA_HEADS * HD), ('dsa_k', HD), ('dsa_v', HD),
    ('idx_q', IDX_HEADS * IDX_DIM), ('idx_k', IDX_DIM), ('idx_w', IDX_HEADS),
    ('nsa_q', NSA_HEADS * HD), ('cmp_k', HD), ('cmp_v', HD), ('sel_k', HD), ('sel_v', HD),
    ('win_k', HD), ('win_v', HD), ('nsa_g', 3 * NSA_HEADS),
    ('mla_cq', D_CQ), ('mla_ckv', D_C), ('mla_kr', D_ROPE),
    ('merge_g', 3 * D_MODEL),
)
IN_W = sum(w for _, w in IN_SPLITS)

kernel_name = 'hybrid_dsa_nsa_mla_decode_step'


def rmsnorm(x, g):
    xf = x.astype(jnp.float32)
    y = xf * lax.rsqrt(jnp.mean(xf * xf, axis=-1, keepdims=True) + EPS)
    return (y * g.astype(jnp.float32)).astype(x.dtype)


def rope(x, pos):
    d = x.shape[-1]
    half = d // 2
    inv = ROPE_THETA ** (-jnp.arange(half, dtype=jnp.float32) * 2.0 / d)
    ang = pos.astype(jnp.float32)[:, None] * inv[None, :]
    cos = jnp.cos(ang)[None, :, None, :]
    sin = jnp.sin(ang)[None, :, None, :]
    xf = x.astype(jnp.float32)
    x1, x2 = xf[..., :half], xf[..., half:]
    return jnp.concatenate([x1 * cos - x2 * sin, x2 * cos + x1 * sin], axis=-1).astype(x.dtype)


def masked_softmax(s, mask):
    return jax.nn.softmax(jnp.where(mask, s, NEG), axis=-1)


def to_blocks(a, g):
    b, t = a.shape[:2]
    return jnp.moveaxis(a.reshape(b, t // g, g, *a.shape[2:]), 1, 0)


def from_blocks(a):
    nb, b, g = a.shape[:3]
    return jnp.moveaxis(a, 0, 1).reshape(b, nb * g, *a.shape[3:])


def split_cols(p):
    out = {}
    off = 0
    for name, w in IN_SPLITS:
        out[name] = p[..., off:off + w]
        off += w
    return out


def project(h, pos, lp):
    b, t, _ = h.shape
    p = split_cols(h @ lp['w_in'])

    def heads(a, n):
        return a.reshape(b, t, n, -1)

    def single(a):
        return rope(a[:, :, None, :], pos)[:, :, 0]

    q_m = (rmsnorm(p['mla_cq'], lp['mla_q_norm_g']) @ lp['w_uq']).reshape(b, t, MLA_HEADS, D_NOPE + D_ROPE)
    return {
        'dsa_q': rope(heads(p['dsa_q'], DSA_HEADS), pos),
        'dsa_kv': jnp.concatenate([single(p['dsa_k']), p['dsa_v']], axis=-1),
        'idx_q': rope(heads(p['idx_q'], IDX_HEADS), pos),
        'idx_k': single(p['idx_k']),
        'idx_w': p['idx_w'] * IDX_HEADS ** -0.5,
        'nsa_q': rope(heads(p['nsa_q'], NSA_HEADS), pos),
        'cmp_kv': jnp.concatenate([single(p['cmp_k']), p['cmp_v']], axis=-1),
        'sel_kv': jnp.concatenate([single(p['sel_k']), p['sel_v']], axis=-1),
        'win_kv': jnp.concatenate([single(p['win_k']), p['win_v']], axis=-1),
        'nsa_g': jax.nn.sigmoid(p['nsa_g'].astype(jnp.float32)).reshape(b, t, 3, NSA_HEADS).astype(h.dtype),
        'mla_qlat': jnp.einsum('bthn,chn->bthc', q_m[..., :D_NOPE], lp['w_uk']),
        'mla_qr': rope(q_m[..., D_NOPE:], pos),
        'mla_ckv': jnp.concatenate([rmsnorm(p['mla_ckv'], lp['mla_kv_norm_g']), single(p['mla_kr'])], axis=-1),
        'merge_g': p['merge_g'].reshape(b, t, 3, D_MODEL),
    }


def attend_shared(q, k, v, mask):
    s = jnp.einsum('bqhd,bkd->bqhk', q, k).astype(jnp.float32) * (q.shape[-1] ** -0.5)
    p = masked_softmax(s, mask[None, :, None, :])
    return jnp.einsum('bqhk,bkd->bqhd', p.astype(v.dtype), v)


def attend_gathered(q, kv, valid):
    s = jnp.einsum('bqhd,bqkd->bqhk', q, kv[..., :HD]).astype(jnp.float32) * (HD ** -0.5)
    p = masked_softmax(s, valid[:, :, None, :])
    return jnp.einsum('bqhk,bqkd->bqhd', p.astype(kv.dtype), kv[..., HD:])


def index_scores(iq, iw, keys):
    s = jax.nn.relu(jnp.einsum('bqhd,bsd->bqhs', iq, keys).astype(jnp.float32) * (IDX_DIM ** -0.5))
    return jnp.einsum('bqh,bqhs->bqs', iw.astype(jnp.float32), s)


def gather_pages(pool, l, page_table, lo, hi):
    b, n_pages = page_table.shape
    return pool[l, page_table, :, lo:hi].reshape(b, n_pages * PAGE_SIZE, hi - lo)


def gather_rows(pool, l, page_table, new_rows, idx, lo, hi):
    bi = jnp.arange(idx.shape[0])[:, None, None]
    pidx = jnp.minimum(idx, PAST_LEN - 1)
    past = pool[l, page_table[bi, pidx // PAGE_SIZE], pidx % PAGE_SIZE, lo:hi]
    new = new_rows[bi, jnp.clip(idx - PAST_LEN, 0, new_rows.shape[1] - 1)]
    return jnp.where((idx >= PAST_LEN)[..., None], new, past)


def compress(kv, cmp_pos, cmp_w1, cmp_w2):
    b = kv.shape[0]
    n = kv.shape[1] // CMP_BLK
    blk = kv.reshape(b, n, CMP_BLK, 2, HD) + cmp_pos
    flat = jnp.moveaxis(blk, 3, 2).reshape(b, n, 2, CMP_BLK * HD)
    hid = jax.nn.gelu(jnp.einsum('bnke,keh->bnkh', flat, cmp_w1))
    out = jnp.einsum('bnkh,khd->bnkd', hid, cmp_w2)
    return out[:, :, 0], out[:, :, 1]


def nsa_cmp_attend(q, ck, cv, tq):
    n = ck.shape[1]
    s = jnp.einsum('bqhd,bnd->bqhn', q, ck).astype(jnp.float32) * (HD ** -0.5)
    vis = (((jnp.arange(n) + 1) * CMP_BLK - 1)[None, :] <= tq[:, None])[None, :, None, :]
    p = masked_softmax(s, vis) * vis
    return jnp.einsum('bqhn,bnd->bqhd', p.astype(cv.dtype), cv), p


def nsa_select(p_cmp, tq, n_blocks):
    b, q, _, n = p_cmp.shape
    r = SEL_BLK // CMP_BLK
    imp = jnp.pad(p_cmp.sum(axis=2), ((0, 0), (0, 0), (0, n_blocks * r - n)))
    imp = imp.reshape(b, q, n_blocks, r).sum(axis=-1)
    j = jnp.arange(n_blocks)[None, :]
    cur = (tq // SEL_BLK)[:, None]
    imp = jnp.where((j == cur) | (j == 0), FORCE, imp)
    imp = jnp.where(j > cur, NEG, imp)
    return lax.top_k(imp, min(N_SEL, n_blocks))[1]


def nsa_combine(g, o_c, o_s, o_w):
    return g[:, :, 0, :, None] * o_c + g[:, :, 1, :, None] * o_s + g[:, :, 2, :, None] * o_w


def merge_out_ffn(x, merge_g, o_dsa, o_nsa, o_mla, lp):
    b, t = x.shape[:2]
    g = jax.nn.sigmoid(merge_g.astype(jnp.float32)).astype(x.dtype)
    mix = (g[:, :, 0] * (o_dsa.reshape(b, t, -1) @ lp['w_br_dsa'])
           + g[:, :, 1] * (o_nsa.reshape(b, t, -1) @ lp['w_br_nsa'])
           + g[:, :, 2] * (o_mla.reshape(b, t, -1) @ lp['w_br_mla']))
    x = x + mix @ lp['w_o']
    h = rmsnorm(x, lp['norm2_g'])
    return x + (jax.nn.silu(h @ lp['w_ff_gate']) * (h @ lp['w_ff_up'])) @ lp['w_ff_down']


def dsa_prompt(pr, pos):
    kv, ik = pr['dsa_kv'], pr['idx_k']
    b, t = kv.shape[:2]
    topk = min(DSA_TOPK, t // 4)
    bi = jnp.arange(b)[:, None, None]

    def body(args):
        qb, iqb, iwb, tq = args
        scores = index_scores(iqb, iwb, ik)
        scores = jnp.where((pos[None, :] <= tq[:, None])[None], scores, NEG)
        sel = lax.top_k(scores, topk)[1]
        return attend_gathered(qb, kv[bi, sel], sel <= tq[None, :, None])

    return from_blocks(lax.map(body, (to_blocks(pr['dsa_q'], Q_BLK), to_blocks(pr['idx_q'], Q_BLK),
                                      to_blocks(pr['idx_w'], Q_BLK), pos.reshape(-1, Q_BLK))))


def nsa_sel_prompt(q, sel_kv, blocks, pos):
    b, t = sel_kv.shape[:2]
    sel_blk = sel_kv.reshape(b, t // SEL_BLK, SEL_BLK, WIN_ROW)
    bi = jnp.arange(b)[:, None, None]

    def body(args):
        qb, bb, tq = args
        g = tq.shape[0]
        kv = sel_blk[bi, bb].reshape(b, g, -1, WIN_ROW)
        kpos = (bb[..., None] * SEL_BLK + jnp.arange(SEL_BLK)).reshape(b, g, -1)
        return attend_gathered(qb, kv, kpos <= tq[None, :, None])

    return from_blocks(lax.map(body, (to_blocks(q, G_BLK), to_blocks(blocks, G_BLK), pos.reshape(-1, G_BLK))))


def window_prompt(q, win_kv):
    t = win_kv.shape[1]
    kvp = jnp.pad(win_kv, ((0, 0), (WINDOW, 0), (0, 0)))
    span = WINDOW + Q_BLK

    def body(args):
        qb, i = args
        kvb = lax.dynamic_slice_in_dim(kvp, i * Q_BLK, span, axis=1)
        kpos = i * Q_BLK - WINDOW + jnp.arange(span)
        tq = i * Q_BLK + jnp.arange(Q_BLK)
        mask = (kpos[None, :] <= tq[:, None]) & (kpos[None, :] >= tq[:, None] - WINDOW) & (kpos[None, :] >= 0)
        return attend_shared(qb, kvb[..., :HD], kvb[..., HD:], mask)

    return from_blocks(lax.map(body, (to_blocks(q, Q_BLK), jnp.arange(t // Q_BLK))))


def mla_prompt(pr, pos, w_uv):
    ckv, kr = pr['mla_ckv'][..., :D_C], pr['mla_ckv'][..., D_C:]
    scale = (D_NOPE + D_ROPE) ** -0.5

    def body(args):
        qb, qrb, tq = args
        s = (jnp.einsum('bqhc,bsc->bqhs', qb, ckv) + jnp.einsum('bqhr,bsr->bqhs', qrb, kr)).astype(jnp.float32) * scale
        p = masked_softmax(s, (pos[None, :] <= tq[:, None])[None, :, None, :])
        return jnp.einsum('bqhs,bsc->bqhc', p.astype(ckv.dtype), ckv)

    lat = from_blocks(lax.map(body, (to_blocks(pr['mla_qlat'], Q_BLK), to_blocks(pr['mla_qr'], Q_BLK),
                                     pos.reshape(-1, Q_BLK))))
    return jnp.einsum('bthc,chv->bthv', lat, w_uv)


def layer_prompt(x, pos, lp):
    t = x.shape[1]
    pr = project(rmsnorm(x, lp['norm1_g']), pos, lp)
    o_dsa = dsa_prompt(pr, pos)
    ck, cv = compress(pr['cmp_kv'], lp['cmp_pos'], lp['cmp_w1'], lp['cmp_w2'])
    o_c, p_c = nsa_cmp_attend(pr['nsa_q'], ck, cv, pos)
    blocks = nsa_select(p_c, pos, t // SEL_BLK)
    o_s = nsa_sel_prompt(pr['nsa_q'], pr['sel_kv'], blocks, pos)
    o_w = window_prompt(pr['nsa_q'], pr['win_kv'])
    o_nsa = nsa_combine(pr['nsa_g'], o_c, o_s, o_w)
    o_mla = mla_prompt(pr, pos, lp['w_uv'])
    x = merge_out_ffn(x, pr['merge_g'], o_dsa, o_nsa, o_mla, lp)
    rows_dsa = jnp.concatenate([pr['dsa_kv'], pr['idx_k']], axis=-1)
    rows_nsa = jnp.concatenate([pr['cmp_kv'], pr['sel_kv']], axis=-1)
    win = pr['win_kv'][:, t - min(WINDOW, t):]
    return x, rows_dsa, rows_nsa, pr['mla_ckv'], win


def layer_sample(x, pos, l, cache_dsa, cache_nsa, cache_mla, win_buf, page_table, lp):
    b, q = x.shape[:2]
    seq_len = PAST_LEN + q
    pr = project(rmsnorm(x, lp['norm1_g']), pos, lp)

    past_ik = gather_pages(cache_dsa, l, page_table, 2 * HD, DSA_ROW)
    scores = jnp.concatenate([index_scores(pr['idx_q'], pr['idx_w'], past_ik),
                              index_scores(pr['idx_q'], pr['idx_w'], pr['idx_k'])], axis=-1)
    scores = jnp.where((jnp.arange(seq_len)[None, :] <= pos[:, None])[None], scores, NEG)
    sel = lax.top_k(scores, min(DSA_TOPK, seq_len // 4))[1]
    kv_sel = gather_rows(cache_dsa, l, page_table, pr['dsa_kv'], sel, 0, 2 * HD)
    o_dsa = attend_gathered(pr['dsa_q'], kv_sel, sel <= pos[None, :, None])

    past_cmp = gather_pages(cache_nsa, l, page_table, 0, 2 * HD)
    n_new = (q // CMP_BLK) * CMP_BLK
    ck_p, cv_p = compress(past_cmp, lp['cmp_pos'], lp['cmp_w1'], lp['cmp_w2'])
    ck_n, cv_n = compress(pr['cmp_kv'][:, :n_new], lp['cmp_pos'], lp['cmp_w1'], lp['cmp_w2'])
    o_c, p_c = nsa_cmp_attend(pr['nsa_q'], jnp.concatenate([ck_p, ck_n], axis=1),
                              jnp.concatenate([cv_p, cv_n], axis=1), pos)
    blocks = nsa_select(p_c, pos, -(-seq_len // SEL_BLK))
    kpos = (blocks[..., None] * SEL_BLK + jnp.arange(SEL_BLK)).reshape(b, q, -1)
    kv_blk = gather_rows(cache_nsa, l, page_table, pr['sel_kv'], kpos, 2 * HD, 4 * HD)
    o_s = attend_gathered(pr['nsa_q'], kv_blk, kpos <= pos[None, :, None])
    w = win_buf.shape[1]
    kv_w = jnp.concatenate([win_buf, pr['win_kv']], axis=1)
    wpos = PAST_LEN - w + jnp.arange(w + q)
    wmask = (wpos[None, :] <= pos[:, None]) & (wpos[None, :] >= pos[:, None] - WINDOW)
    o_w = attend_shared(pr['nsa_q'], kv_w[..., :HD], kv_w[..., HD:], wmask)
    o_nsa = nsa_combine(pr['nsa_g'], o_c, o_s, o_w)

    past = gather_pages(cache_mla, l, page_table, 0, MLA_ROW)
    new = pr['mla_ckv']

    def mla_scores(rows):
        return (jnp.einsum('bqhc,bsc->bqhs', pr['mla_qlat'], rows[..., :D_C])
                + jnp.einsum('bqhr,bsr->bqhs', pr['mla_qr'], rows[..., D_C:]))

    s = jnp.concatenate([mla_scores(past), mla_scores(new)], axis=-1).astype(jnp.float32) * ((D_NOPE + D_ROPE) ** -0.5)
    p = masked_softmax(s, (jnp.arange(seq_len)[None, :] <= pos[:, None])[None, :, None, :]).astype(new.dtype)
    lat = (jnp.einsum('bqhs,bsc->bqhc', p[..., :PAST_LEN], past[..., :D_C])
           + jnp.einsum('bqhs,bsc->bqhc', p[..., PAST_LEN:], new[..., :D_C]))
    o_mla = jnp.einsum('bqhc,chv->bqhv', lat, lp['w_uv'])

    x = merge_out_ffn(x, pr['merge_g'], o_dsa, o_nsa, o_mla, lp)
    rows_dsa = jnp.concatenate([pr['dsa_kv'], pr['idx_k']], axis=-1)
    rows_nsa = jnp.concatenate([pr['cmp_kv'], pr['sel_kv']], axis=-1)
    return x, rows_dsa, rows_nsa, new, kv_w[:, q:]


def setup_inputs(seed: int = 0) -> dict:
    key = jax.random.key(seed)
    ks = jax.random.split(key, 28)
    f32 = jnp.float32
    n_pages = PAST_LEN // PAGE_SIZE
    n_pool = (DEC_BATCH * n_pages * 5) // 4
    w_buf = min(WINDOW, PAST_LEN)

    def nrm(k, shape, scale):
        return jax.random.normal(k, shape, f32) * scale

    def gain(k, shape):
        return 1.0 + 0.01 * jax.random.normal(k, shape, f32)

    page_table = jax.random.permutation(ks[6], n_pool)[:DEC_BATCH * n_pages].reshape(DEC_BATCH, n_pages).astype(jnp.int32)
    return {
        'x_prompt': nrm(ks[0], (BATCH, SEQ, D_MODEL), 1.0),
        'x_sample': nrm(ks[1], (DEC_BATCH, DEC_SEQ, D_MODEL), 1.0),
        'cache_dsa': nrm(ks[2], (DEPTH, n_pool, PAGE_SIZE, DSA_ROW), 1.0),
        'cache_nsa': nrm(ks[3], (DEPTH, n_pool, PAGE_SIZE, NSA_ROW), 1.0),
        'cache_mla': nrm(ks[4], (DEPTH, n_pool, PAGE_SIZE, MLA_ROW), 1.0),
        'state_nsa_win': nrm(ks[5], (DEPTH, DEC_BATCH, w_buf, WIN_ROW), 1.0),
        'page_table': page_table,
        'norm1_g': gain(ks[7], (DEPTH, D_MODEL)),
        'w_in': nrm(ks[8], (DEPTH, D_MODEL, IN_W), D_MODEL ** -0.5),
        'mla_q_norm_g': gain(ks[9], (DEPTH, D_CQ)),
        'w_uq': nrm(ks[10], (DEPTH, D_CQ, MLA_HEADS * (D_NOPE + D_ROPE)), D_CQ ** -0.5),
        'mla_kv_norm_g': gain(ks[11], (DEPTH, D_C)),
        'w_uk': nrm(ks[12], (DEPTH, D_C, MLA_HEADS, D_NOPE), D_C ** -0.5),
        'w_uv': nrm(ks[13], (DEPTH, D_C, MLA_HEADS, D_V), D_C ** -0.5),
        'cmp_pos': nrm(ks[14], (DEPTH, CMP_BLK, 2, HD), 0.1),
        'cmp_w1': nrm(ks[15], (DEPTH, 2, CMP_BLK * HD, CMP_HID), (CMP_BLK * HD) ** -0.5),
        'cmp_w2': nrm(ks[16], (DEPTH, 2, CMP_HID, HD), CMP_HID ** -0.5),
        'w_br_dsa': nrm(ks[17], (DEPTH, DSA_HEADS * HD, D_MODEL), (DSA_HEADS * HD) ** -0.5),
        'w_br_nsa': nrm(ks[18], (DEPTH, NSA_HEADS * HD, D_MODEL), (NSA_HEADS * HD) ** -0.5),
        'w_br_mla': nrm(ks[19], (DEPTH, MLA_HEADS * D_V, D_MODEL), (MLA_HEADS * D_V) ** -0.5),
        'w_o': nrm(ks[20], (DEPTH, D_MODEL, D_MODEL), D_MODEL ** -0.5),
        'norm2_g': gain(ks[21], (DEPTH, D_MODEL)),
        'w_ff_gate': nrm(ks[22], (DEPTH, D_MODEL, D_FF), D_MODEL ** -0.5),
        'w_ff_up': nrm(ks[23], (DEPTH, D_MODEL, D_FF), D_MODEL ** -0.5),
        'w_ff_down': nrm(ks[24], (DEPTH, D_FF, D_MODEL), D_FF ** -0.5),
        'final_norm_g': gain(ks[25], (D_MODEL,)),
    }


def reference(x_prompt, x_sample, cache_dsa, cache_nsa, cache_mla, state_nsa_win, page_table,
              norm1_g, w_in, mla_q_norm_g, w_uq, mla_kv_norm_g, w_uk, w_uv, cmp_pos, cmp_w1, cmp_w2,
              w_br_dsa, w_br_nsa, w_br_mla, w_o, norm2_g, w_ff_gate, w_ff_up, w_ff_down, final_norm_g):
    pos_p = jnp.arange(x_prompt.shape[1], dtype=jnp.int32)
    pos_s = PAST_LEN + jnp.arange(x_sample.shape[1], dtype=jnp.int32)
    xp, xs = x_prompt, x_sample
    dsa_p, dsa_s, nsa_p, nsa_s, mla_p, mla_s, win_p, win_s = [], [], [], [], [], [], [], []
    for l in range(DEPTH):
        lp = {
            'norm1_g': norm1_g[l], 'w_in': w_in[l], 'mla_q_norm_g': mla_q_norm_g[l], 'w_uq': w_uq[l],
            'mla_kv_norm_g': mla_kv_norm_g[l], 'w_uk': w_uk[l], 'w_uv': w_uv[l], 'cmp_pos': cmp_pos[l],
            'cmp_w1': cmp_w1[l], 'cmp_w2': cmp_w2[l], 'w_br_dsa': w_br_dsa[l], 'w_br_nsa': w_br_nsa[l],
            'w_br_mla': w_br_mla[l], 'w_o': w_o[l], 'norm2_g': norm2_g[l], 'w_ff_gate': w_ff_gate[l],
            'w_ff_up': w_ff_up[l], 'w_ff_down': w_ff_down[l],
        }
        xp, r_dsa, r_nsa, r_mla, r_win = layer_prompt(xp, pos_p, lp)
        dsa_p.append(r_dsa); nsa_p.append(r_nsa); mla_p.append(r_mla); win_p.append(r_win)
        xs, r_dsa, r_nsa, r_mla, r_win = layer_sample(xs, pos_s, l, cache_dsa, cache_nsa, cache_mla,
                                                      state_nsa_win[l], page_table, lp)
        dsa_s.append(r_dsa); nsa_s.append(r_nsa); mla_s.append(r_mla); win_s.append(r_win)
    y_prompt = rmsnorm(xp, final_norm_g)
    y_sample = rmsnorm(xs, final_norm_g)
    return (y_prompt, y_sample, jnp.stack(dsa_p), jnp.stack(dsa_s), jnp.stack(nsa_p), jnp.stack(nsa_s),
            jnp.stack(mla_p), jnp.stack(mla_s), jnp.stack(win_p), jnp.stack(win_s))
```

```python
import functools
import struct

import jax
import jax.numpy as jnp
from jax import lax
from jax.experimental import pallas as pl
from jax.experimental.pallas import tpu as pltpu

F32 = jnp.float32
BF16 = jnp.bfloat16

HD = 64
HEADS = 8
IDX_DIM = 64
DSA_TOPK = 256
CMP_BLK = 32
CMP_HID = 2 * HD
SEL_BLK = 64
N_SEL = 16
WINDOW = 512
D_CQ = 256
D_C = 128
D_NOPE = 64
D_ROPE = 32
D_V = 64
ROPE_THETA = 10000.0
EPS = 1e-6
NEG = -1e30
FORCE = 1e9
PAGE_SIZE = 128

DSA_ROW = 2 * HD + IDX_DIM
NSA_ROW = 4 * HD
MLA_ROW = D_C + D_ROPE
WIN_ROW = 2 * HD

IN_SPLITS = (
    ('dsa_q', HEADS * HD), ('dsa_k', HD), ('dsa_v', HD),
    ('idx_q', HEADS * IDX_DIM), ('idx_k', IDX_DIM), ('idx_w', HEADS),
    ('nsa_q', HEADS * HD), ('cmp_k', HD), ('cmp_v', HD), ('sel_k', HD), ('sel_v', HD),
    ('win_k', HD), ('win_v', HD), ('nsa_g', 3 * HEADS),
    ('mla_cq', D_CQ), ('mla_ckv', D_C), ('mla_kr', D_ROPE),
)

LANES = 128
V7X_VMEM_LIMIT = 56 * 1024 * 1024

Q_BLK = 128
KV_CHUNK = 512
ROPE_W = 15 * LANES
PLAIN_W = 6 * LANES


def _f32_key(v):
    bits = struct.unpack('<i', struct.pack('<f', v))[0]
    return bits ^ ((bits >> 31) & 0x7FFFFFFF)


NEG_KEY = _f32_key(NEG)
INT_MIN = -2 ** 31


def _sort_key(x):
    bits = pltpu.bitcast(x + 0.0, jnp.int32)
    return bits ^ ((bits >> 31) & 0x7FFFFFFF)


def _nt_dot(a, b):
    return lax.dot_general(a, b, (((1,), (1,)), ((), ())), preferred_element_type=F32)


def _dot(a, b):
    return jnp.dot(a, b, preferred_element_type=F32)


def _rope128(x, c, sa, sb, half):
    return x * c + pltpu.roll(x, LANES - half, 1) * sa + pltpu.roll(x, half, 1) * sb


def _cparams(n_axes):
    return pltpu.CompilerParams(dimension_semantics=("arbitrary",) * n_axes, vmem_limit_bytes=V7X_VMEM_LIMIT)


def _in_proj_body(x_ref, g1_ref, wr_ref, wp_ref, c64_ref, sa64_ref, sb64_ref, c32_ref, sa32_ref, sb32_ref,
                  gq_ref, wuq_ref, wuk_ref, gkv_ref,
                  q3_ref, qcat_ref, idxw_ref, nsag_ref, rdsa_ref, rnsa_ref, rmla_ref, win_ref, kv7_ref, mlabf_ref):
    x = x_ref[...]
    h = (x * lax.rsqrt(jnp.mean(x * x, axis=-1, keepdims=True) + EPS) * g1_ref[...]).astype(BF16)
    a = _dot(h, wr_ref[...])
    b = _dot(h, wp_ref[...])
    c64, sa64, sb64 = c64_ref[...], sa64_ref[...], sb64_ref[...]
    c32, sa32, sb32 = c32_ref[...], sa32_ref[...], sb32_ref[...]
    rg = [_rope128(a[:, g * LANES:(g + 1) * LANES], c64, sa64, sb64, HD // 2) for g in range(ROPE_W // LANES)]
    for grp in range(3):
        for hh in range(HEADS):
            g, o = grp * 4 + hh // 2, (hh % 2) * HD
            q3_ref[grp, hh] = rg[g][:, o:o + HD].astype(BF16)
    dsa_k, idx_k = rg[12][:, :HD], rg[12][:, HD:]
    cmp_k, sel_k = rg[13][:, :HD], rg[13][:, HD:]
    win_k = rg[14][:, :HD]
    dsa_v, cmp_v, sel_v, win_v = (b[:, i * HD:(i + 1) * HD] for i in range(4))
    rdsa_ref[:, 0:HD] = dsa_k
    rdsa_ref[:, HD:2 * HD] = dsa_v
    rdsa_ref[:, 2 * HD:3 * HD] = idx_k
    rnsa_ref[:, 0:HD] = cmp_k
    rnsa_ref[:, HD:2 * HD] = cmp_v
    rnsa_ref[:, 2 * HD:3 * HD] = sel_k
    rnsa_ref[:, 3 * HD:4 * HD] = sel_v
    win_ref[:, 0:HD] = win_k
    win_ref[:, HD:2 * HD] = win_v
    for i, t in enumerate((dsa_k, dsa_v, idx_k, sel_k, sel_v, win_k, win_v)):
        kv7_ref[i] = t.astype(BF16)
    ckv = b[:, 2 * LANES:3 * LANES]
    ckv = ckv * lax.rsqrt(jnp.mean(ckv * ckv, axis=-1, keepdims=True) + EPS) * gkv_ref[...]
    tail = b[:, 5 * LANES:6 * LANES]
    kr = _rope128(tail, c32, sa32, sb32, D_ROPE // 2)[:, :D_ROPE]
    rmla_ref[:, 0:D_C] = ckv
    rmla_ref[:, D_C:MLA_ROW] = kr
    mlabf_ref[:, 0:D_C] = ckv.astype(BF16)
    mlabf_ref[:, D_C:MLA_ROW] = kr.astype(BF16)
    idxw_ref[...] = tail[:, D_ROPE:D_ROPE + HEADS] * (HEADS ** -0.5)
    nsag_ref[...] = jax.nn.sigmoid(tail[:, D_ROPE + HEADS:D_ROPE + 4 * HEADS])
    cq = b[:, 3 * LANES:5 * LANES]
    cq = (cq * lax.rsqrt(jnp.mean(cq * cq, axis=-1, keepdims=True) + EPS) * gq_ref[...]).astype(BF16)
    qm = _dot(cq, wuq_ref[...])
    qlat = _dot(qm[:, :HEADS * D_NOPE].astype(BF16), wuk_ref[...])
    qr = [_rope128(qm[:, HEADS * D_NOPE + g * LANES:HEADS * D_NOPE + (g + 1) * LANES], c32, sa32, sb32, D_ROPE // 2)
          for g in range(HEADS * D_ROPE // LANES)]
    per = LANES // D_ROPE
    for hh in range(HEADS):
        qcat_ref[hh, :, 0:D_C] = qlat[:, hh * D_C:(hh + 1) * D_C].astype(BF16)
        o = (hh % per) * D_ROPE
        qcat_ref[hh, :, D_C:MLA_ROW] = qr[hh // per][:, o:o + D_ROPE].astype(BF16)


def _rope_tables(pos, d):
    half = d // 2
    inv = ROPE_THETA ** (-jnp.arange(half, dtype=F32) * 2.0 / d)
    ang = pos.astype(F32)[:, None] * inv[None, :]
    cos, sin = jnp.cos(ang), jnp.sin(ang)
    z = jnp.zeros_like(sin)
    rep = LANES // d
    return (jnp.tile(jnp.concatenate([cos, cos], 1), (1, rep)),
            jnp.tile(jnp.concatenate([-sin, z], 1), (1, rep)),
            jnp.tile(jnp.concatenate([z, sin], 1), (1, rep)))


def _split_w_in(w_in_l):
    cols, off = {}, 0
    for name, w in IN_SPLITS:
        cols[name] = w_in_l[:, off:off + w]
        off += w
    cols['merge_g'] = w_in_l[:, off:]
    return cols


def _prep_layer(lw):
    c = _split_w_in(lw['w_in'])
    d = lw['w_in'].shape[0]
    z64 = jnp.zeros((d, HD), F32)
    wr = jnp.concatenate([c['dsa_q'], c['idx_q'], c['nsa_q'], c['dsa_k'], c['idx_k'], c['cmp_k'], c['sel_k'],
                          c['win_k'], z64], axis=1)
    wp = jnp.concatenate([c['dsa_v'], c['cmp_v'], c['sel_v'], c['win_v'], c['mla_ckv'], c['mla_cq'],
                          c['mla_kr'], c['idx_w'], c['nsa_g'], z64], axis=1)
    assert wr.shape[1] == ROPE_W and wp.shape[1] == PLAIN_W
    wuq = lw['w_uq'].reshape(D_CQ, HEADS, D_NOPE + D_ROPE)
    wuq = jnp.concatenate([wuq[:, :, :D_NOPE].reshape(D_CQ, HEADS * D_NOPE),
                           wuq[:, :, D_NOPE:].reshape(D_CQ, HEADS * D_ROPE)], axis=1)
    wk = jnp.transpose(lw['w_uk'], (1, 2, 0))
    wuk_bd = (jnp.eye(HEADS, dtype=F32)[:, None, :, None] * wk[:, :, None, :]).reshape(HEADS * D_NOPE, HEADS * D_C)
    wuv = jnp.transpose(lw['w_uv'], (1, 0, 2))
    w1 = lw['cmp_w1'].reshape(2, CMP_BLK, HD, CMP_HID)
    zz = jnp.zeros((CMP_BLK, HD, CMP_HID), F32)
    w1bd = jnp.concatenate([jnp.concatenate([w1[0], zz], axis=2), jnp.concatenate([zz, w1[1]], axis=2)], axis=1)
    z2 = jnp.zeros((CMP_HID, HD), F32)
    w2bd = jnp.concatenate([jnp.concatenate([lw['cmp_w2'][0], z2], axis=1),
                            jnp.concatenate([z2, lw['cmp_w2'][1]], axis=1)], axis=0)
    return dict(
        g1=lw['norm1_g'][None, :], wr=wr.astype(BF16), wp=wp.astype(BF16),
        gq=lw['mla_q_norm_g'][None, :], wuq=wuq.astype(BF16), wuk=wuk_bd.astype(BF16),
        gkv=lw['mla_kv_norm_g'][None, :], wuv=wuv.astype(BF16),
        cmp_pos=lw['cmp_pos'].reshape(CMP_BLK, 2 * HD), w1bd=w1bd.astype(BF16), w2bd=w2bd.astype(BF16),
        wmg=c['merge_g'].astype(BF16),
        wbr=jnp.stack([lw['w_br_dsa'], lw['w_br_nsa'], lw['w_br_mla']]).astype(BF16),
        wo=lw['w_o'].astype(BF16), g2=lw['norm2_g'][None, :],
        wg=lw['w_ff_gate'].astype(BF16), wu=lw['w_ff_up'].astype(BF16), wd=lw['w_ff_down'].astype(BF16),
    )


def _in_proj(x2d, pw, tabs64, tabs32, tm, n_pos_blocks):
    n, d = x2d.shape
    full = lambda shape: pl.BlockSpec(shape, lambda i: (0,) * len(shape))
    tab = pl.BlockSpec((tm, LANES), lambda i: (i % n_pos_blocks, 0))
    row = lambda w: pl.BlockSpec((tm, w), lambda i: (i, 0))
    out_shape = (
        jax.ShapeDtypeStruct((3, HEADS, n, HD), BF16),
        jax.ShapeDtypeStruct((HEADS, n, MLA_ROW), BF16),
        jax.ShapeDtypeStruct((n, HEADS), F32),
        jax.ShapeDtypeStruct((n, 3 * HEADS), F32),
        jax.ShapeDtypeStruct((n, DSA_ROW), F32),
        jax.ShapeDtypeStruct((n, NSA_ROW), F32),
        jax.ShapeDtypeStruct((n, MLA_ROW), F32),
        jax.ShapeDtypeStruct((n, WIN_ROW), F32),
        jax.ShapeDtypeStruct((7, n, HD), BF16),
        jax.ShapeDtypeStruct((n, MLA_ROW), BF16),
    )
    out_specs = (
        pl.BlockSpec((3, HEADS, tm, HD), lambda i: (0, 0, i, 0)),
        pl.BlockSpec((HEADS, tm, MLA_ROW), lambda i: (0, i, 0)),
        row(HEADS), row(3 * HEADS), row(DSA_ROW), row(NSA_ROW), row(MLA_ROW), row(WIN_ROW),
        pl.BlockSpec((7, tm, HD), lambda i: (0, i, 0)),
        row(MLA_ROW),
    )
    in_specs = [row(d), full((1, d)), full((d, ROPE_W)), full((d, PLAIN_W)), tab, tab, tab, tab, tab, tab,
                full((1, D_CQ)), full((D_CQ, HEADS * (D_NOPE + D_ROPE))), full((HEADS * D_NOPE, HEADS * D_C)),
                full((1, D_C))]
    return pl.pallas_call(
        _in_proj_body, grid=(n // tm,), in_specs=in_specs, out_specs=out_specs, out_shape=out_shape,
        compiler_params=_cparams(1), name="in_proj",
    )(x2d, pw['g1'], pw['wr'], pw['wp'], *tabs64, *tabs32, pw['gq'], pw['wuq'], pw['wuk'], pw['gkv'])


def _online_step(s, v, m, l, acc):
    m_new = jnp.maximum(m, jnp.max(s, axis=1, keepdims=True))
    alpha = jnp.exp(m - m_new)
    p = jnp.exp(s - m_new)
    l = alpha * l + jnp.sum(p, axis=1, keepdims=True)
    acc = alpha * acc + _dot(p.astype(BF16), v)
    return m_new, l, acc


def _heads_to_lanes(o, q):
    return jnp.concatenate([o[hh * q:(hh + 1) * q] for hh in range(HEADS)], axis=1)


def _count(mask):
    return jnp.sum(jnp.where(mask, 1.0, 0.0), axis=1, keepdims=True)


def _kth_largest_key(key_ref, k):
    kf = float(k)
    base = jnp.where(_count(key_ref[...] >= 0) >= kf, 0, INT_MIN).astype(jnp.int32)

    def body(i, base):
        cand = base | lax.shift_left(jnp.int32(1), 30 - i)
        return jnp.where(_count(key_ref[...] >= cand) >= kf, cand, base)

    return lax.fori_loop(0, 31, body, base)


def _prefix_upper(n):
    r = lax.broadcasted_iota(jnp.int32, (n, n), 0)
    c = lax.broadcasted_iota(jnp.int32, (n, n), 1)
    return jnp.where(r <= c, 1.0, 0.0).astype(BF16)


def _dsa_prompt_body(iq_ref, iw_ref, q_ref, ik_ref, k_ref, v_ref, o_ref, sc_ref, key_ref, *, t_len, topk):
    q_blk = Q_BLK
    qi = pl.program_id(1)
    n_chunks = (qi * q_blk + q_blk + KV_CHUNK - 1) // KV_CHUNK
    t = qi * q_blk + lax.broadcasted_iota(jnp.int32, (q_blk, 1), 0)
    lane = lax.broadcasted_iota(jnp.int32, (1, KV_CHUNK), 1)
    iq = iq_ref[...].reshape(HEADS * q_blk, IDX_DIM)
    w = iw_ref[...] * (IDX_DIM ** -0.5)

    sc_ref[...] = jnp.full(sc_ref.shape, NEG, F32)

    def idx_body(c, carry):
        ks = pl.multiple_of(c * KV_CHUNK, KV_CHUNK)
        s = jnp.maximum(_nt_dot(iq, ik_ref[pl.ds(ks, KV_CHUNK), :]), 0.0).reshape(HEADS, q_blk, KV_CHUNK)
        tot = w[:, 0:1] * s[0]
        for hh in range(1, HEADS):
            tot = tot + w[:, hh:hh + 1] * s[hh]
        sc_ref[:, pl.ds(ks, KV_CHUNK)] = jnp.where(ks + lane <= t, tot, NEG)
        return carry

    lax.fori_loop(0, n_chunks, idx_body, 0)

    key_ref[...] = _sort_key(sc_ref[...])
    thr = _kth_largest_key(key_ref, topk)
    key = key_ref[...]
    n_gt = _count(key > thr)
    n_eq = _count(key == thr)
    sc_ref[...] = jnp.where(key >= thr, 0.0, NEG)
    overflow = jnp.where((n_gt + n_eq > float(topk)) & (thr > NEG_KEY), 1.0, 0.0)

    @pl.when(jnp.max(overflow) > 0.0)
    def _():
        pw = 256
        upper = _prefix_upper(pw)
        quota = float(topk) - n_gt
        carry = jnp.zeros((q_blk, 1), F32)
        for c in range(t_len // pw):
            kc = key_ref[:, c * pw:(c + 1) * pw]
            eq = kc == thr
            pre = _dot(jnp.where(eq, 1.0, 0.0).astype(BF16), upper) + carry
            sel = (kc > thr) | (eq & (pre <= quota))
            sc_ref[:, c * pw:(c + 1) * pw] = jnp.where(sel, 0.0, NEG)
            carry = pre[:, pw - 1:pw]

    q = q_ref[...].reshape(HEADS * q_blk, HD)

    def att_body(c, carry):
        ks = pl.multiple_of(c * KV_CHUNK, KV_CHUNK)
        bias = jnp.where(ks + lane <= t, sc_ref[:, pl.ds(ks, KV_CHUNK)], NEG)
        s = _nt_dot(q, k_ref[pl.ds(ks, KV_CHUNK), :]) * (HD ** -0.5)
        s = (s.reshape(HEADS, q_blk, KV_CHUNK) + bias[None]).reshape(HEADS * q_blk, KV_CHUNK)
        return _online_step(s, v_ref[pl.ds(ks, KV_CHUNK), :], *carry)

    init = (jnp.full((HEADS * q_blk, 1), 2 * NEG, F32), jnp.zeros((HEADS * q_blk, 1), F32),
            jnp.zeros((HEADS * q_blk, HD), F32))
    _, l, acc = lax.fori_loop(0, n_chunks, att_body, init)
    o_ref[...] = _heads_to_lanes(acc / l, q_blk).astype(BF16)


def _dsa_prompt(q3, idx_w, kv7, bsz, t_len):
    n = bsz * t_len
    nq = t_len // Q_BLK
    topk = min(DSA_TOPK, t_len // 4)
    qspec = lambda g: pl.BlockSpec((None, HEADS, Q_BLK, HD), lambda b, i: (g, 0, b * nq + i, 0))
    kspec = lambda j: pl.BlockSpec((None, t_len, HD), lambda b, i: (j, b, 0))
    return pl.pallas_call(
        functools.partial(_dsa_prompt_body, t_len=t_len, topk=topk),
        grid=(bsz, nq),
        in_specs=[qspec(1), pl.BlockSpec((Q_BLK, HEADS), lambda b, i: (b * nq + i, 0)), qspec(0),
                  kspec(2), kspec(0), kspec(1)],
        out_specs=pl.BlockSpec((Q_BLK, HEADS * HD), lambda b, i: (b * nq + i, 0)),
        out_shape=jax.ShapeDtypeStruct((n, HEADS * HD), BF16),
        scratch_shapes=[pltpu.VMEM((Q_BLK, t_len), F32), pltpu.VMEM((Q_BLK, t_len), jnp.int32)],
        compiler_params=_cparams(2), name="dsa_prompt",
    )(q3, idx_w, q3, kv7, kv7, kv7)


def _mla_prompt_body(q_ref, k_ref, wuv_ref, o_ref):
    q_blk = Q_BLK
    qi = pl.program_id(1)
    n_chunks = (qi * q_blk + q_blk + KV_CHUNK - 1) // KV_CHUNK
    t = qi * q_blk + lax.broadcasted_iota(jnp.int32, (q_blk, 1), 0)
    lane = lax.broadcasted_iota(jnp.int32, (1, KV_CHUNK), 1)
    q = q_ref[...].reshape(HEADS * q_blk, MLA_ROW)
    scale = (D_NOPE + D_ROPE) ** -0.5

    def body(c, carry):
        ks = pl.multiple_of(c * KV_CHUNK, KV_CHUNK)
        rows = k_ref[pl.ds(ks, KV_CHUNK), :]
        bias = jnp.where(ks + lane <= t, 0.0, NEG)
        s = _nt_dot(q, rows) * scale
        s = (s.reshape(HEADS, q_blk, KV_CHUNK) + bias[None]).reshape(HEADS * q_blk, KV_CHUNK)
        return _online_step(s, rows[:, :D_C], *carry)

    init = (jnp.full((HEADS * q_blk, 1), 2 * NEG, F32), jnp.zeros((HEADS * q_blk, 1), F32),
            jnp.zeros((HEADS * q_blk, D_C), F32))
    _, l, acc = lax.fori_loop(0, n_chunks, body, init)
    lat = (acc / l).astype(BF16)
    o_ref[...] = jnp.concatenate(
        [_dot(lat[hh * q_blk:(hh + 1) * q_blk], wuv_ref[hh]) for hh in range(HEADS)], axis=1).astype(BF16)


def _mla_prompt(qcat, mla_bf, wuv, bsz, t_len):
    n = bsz * t_len
    nq = t_len // Q_BLK
    return pl.pallas_call(
        _mla_prompt_body, grid=(bsz, nq),
        in_specs=[pl.BlockSpec((HEADS, Q_BLK, MLA_ROW), lambda b, i: (0, b * nq + i, 0)),
                  pl.BlockSpec((t_len, MLA_ROW), lambda b, i: (b, 0)),
                  pl.BlockSpec((HEADS, D_C, D_V), lambda b, i: (0, 0, 0))],
        out_specs=pl.BlockSpec((Q_BLK, HEADS * D_V), lambda b, i: (b * nq + i, 0)),
        out_shape=jax.ShapeDtypeStruct((n, HEADS * D_V), BF16),
        compiler_params=_cparams(2), name="mla_prompt",
    )(qcat, mla_bf, wuv)


def _compress_blocks(read_rows, pos_ref, w1_ref, w2_ref, n_blk):
    acc = jnp.zeros((n_blk, 2 * CMP_HID), F32)
    for r in range(CMP_BLK):
        xr = (read_rows(r) + pos_ref[r:r + 1, :]).astype(BF16)
        acc = acc + _dot(xr, w1_ref[r])
    hid = jax.nn.gelu(acc).astype(BF16)
    return _dot(hid, w2_ref[...])


def _cmp_attend(q, ckv, vis):
    s = _nt_dot(q, ckv[:, :HD].astype(BF16)) * (HD ** -0.5)
    s = jnp.where(vis, s, NEG)
    e = jnp.exp(s - jnp.max(s, axis=1, keepdims=True))
    p = jnp.where(vis, e / jnp.sum(e, axis=1, keepdims=True), 0.0)
    return _dot(p.astype(BF16), ckv[:, HD:].astype(BF16)), p


def _nsa_prompt_body(q_ref, g_ref, rows_ref, sk_ref, sv_ref, wk_ref, wv_ref, pos_ref, w1_ref, w2_ref,
                     o_ref, ckv_ref, exp_ref, bias_ref, *, t_len):
    q_blk = Q_BLK
    n_cmp = t_len // CMP_BLK
    n_sel_blk = t_len // SEL_BLK
    ratio = SEL_BLK // CMP_BLK
    n_keep = min(N_SEL, n_sel_blk)
    qi = pl.program_id(1)
    t = qi * q_blk + lax.broadcasted_iota(jnp.int32, (q_blk, 1), 0)

    @pl.when(qi == 0)
    def _():
        ckv_ref[...] = _compress_blocks(lambda r: rows_ref[pl.ds(r, n_cmp, stride=CMP_BLK), :],
                                        pos_ref, w1_ref, w2_ref, n_cmp)
        j = lax.broadcasted_iota(jnp.int32, (n_cmp, t_len), 0)
        s = lax.broadcasted_iota(jnp.int32, (n_cmp, t_len), 1)
        exp_ref[...] = jnp.where(j == (s // SEL_BLK) * ratio, 1.0, 0.0).astype(BF16)

    q = q_ref[...].reshape(HEADS * q_blk, HD)
    g = g_ref[...]

    jc = lax.broadcasted_iota(jnp.int32, (1, n_cmp), 1)
    vis = (jc + 1) * CMP_BLK - 1 <= t
    vis8 = jnp.broadcast_to(vis[None], (HEADS, q_blk, n_cmp)).reshape(HEADS * q_blk, n_cmp)
    o_c, p = _cmp_attend(q, ckv_ref[...], vis8)
    imp = p[0:q_blk]
    for hh in range(1, HEADS):
        imp = imp + p[hh * q_blk:(hh + 1) * q_blk]
    assert ratio == 2
    width = -(-n_cmp // LANES) * LANES
    if width > n_cmp:
        imp = jnp.concatenate([imp, jnp.zeros((q_blk, width - n_cmp), F32)], axis=1)
    val = imp + pltpu.roll(imp, width - 1, 1)
    lw = lax.broadcasted_iota(jnp.int32, (1, width), 1)
    jb = lw // ratio
    cur = t // SEL_BLK
    val = jnp.where((jb == cur) | (jb == 0), FORCE, val)
    val = jnp.where(jb > cur, NEG, val)
    valid = (lw % ratio == 0) & (lw < n_cmp)
    val = jnp.where(valid, val, -jnp.inf)
    rank = jnp.zeros((q_blk, width), F32)
    for i in range(0, n_cmp, ratio):
        col = val[:, i:i + 1]
        rank = rank + jnp.where(lw > i, jnp.where(col >= val, 1.0, 0.0), jnp.where(col > val, 1.0, 0.0))
    selmask = jnp.where(valid & (rank < float(n_keep)), 1.0, 0.0)[:, :n_cmp].astype(BF16)
    bias_ref[...] = jnp.where(_dot(selmask, exp_ref[...]) > 0.5, 0.0, NEG)

    n_chunks = (qi * q_blk + q_blk + KV_CHUNK - 1) // KV_CHUNK
    lane = lax.broadcasted_iota(jnp.int32, (1, KV_CHUNK), 1)

    def sel_body(c, carry):
        ks = pl.multiple_of(c * KV_CHUNK, KV_CHUNK)
        bias = jnp.where(ks + lane <= t, bias_ref[:, pl.ds(ks, KV_CHUNK)], NEG)
        s = _nt_dot(q, sk_ref[pl.ds(ks, KV_CHUNK), :]) * (HD ** -0.5)
        s = (s.reshape(HEADS, q_blk, KV_CHUNK) + bias[None]).reshape(HEADS * q_blk, KV_CHUNK)
        return _online_step(s, sv_ref[pl.ds(ks, KV_CHUNK), :], *carry)

    init = (jnp.full((HEADS * q_blk, 1), 2 * NEG, F32), jnp.zeros((HEADS * q_blk, 1), F32),
            jnp.zeros((HEADS * q_blk, HD), F32))
    _, l, acc = lax.fori_loop(0, n_chunks, sel_body, init)
    o_s = acc / l

    wk_len = min(WINDOW + q_blk, t_len)
    start = pl.multiple_of(jnp.clip(qi * q_blk - WINDOW, 0, t_len - wk_len), LANES)
    kpos = start + lax.broadcasted_iota(jnp.int32, (1, wk_len), 1)
    wbias = jnp.where((kpos <= t) & (kpos >= t - WINDOW), 0.0, NEG)
    s = _nt_dot(q, wk_ref[pl.ds(start, wk_len), :]) * (HD ** -0.5)
    s = (s.reshape(HEADS, q_blk, wk_len) + wbias[None]).reshape(HEADS * q_blk, wk_len)
    e = jnp.exp(s - jnp.max(s, axis=1, keepdims=True))
    o_w = _dot(e.astype(BF16), wv_ref[pl.ds(start, wk_len), :]) / jnp.sum(e, axis=1, keepdims=True)

    outs = []
    for hh in range(HEADS):
        r = slice(hh * q_blk, (hh + 1) * q_blk)
        outs.append(g[:, hh:hh + 1] * o_c[r] + g[:, HEADS + hh:HEADS + hh + 1] * o_s[r]
                    + g[:, 2 * HEADS + hh:2 * HEADS + hh + 1] * o_w[r])
    o_ref[...] = jnp.concatenate(outs, axis=1).astype(BF16)


def _nsa_prompt(q3, nsa_g, rows_nsa, kv7, pw, bsz, t_len):
    n = bsz * t_len
    nq = t_len // Q_BLK
    n_cmp = t_len // CMP_BLK
    kspec = lambda j: pl.BlockSpec((None, t_len, HD), lambda b, i: (j, b, 0))
    full = lambda shape: pl.BlockSpec(shape, lambda b, i: (0,) * len(shape))
    return pl.pallas_call(
        functools.partial(_nsa_prompt_body, t_len=t_len),
        grid=(bsz, nq),
        in_specs=[pl.BlockSpec((None, HEADS, Q_BLK, HD), lambda b, i: (2, 0, b * nq + i, 0)),
                  pl.BlockSpec((Q_BLK, 3 * HEADS), lambda b, i: (b * nq + i, 0)),
                  pl.BlockSpec((t_len, 2 * HD), lambda b, i: (b, 0)),
                  kspec(3), kspec(4), kspec(5), kspec(6),
                  full((CMP_BLK, 2 * HD)), full((CMP_BLK, 2 * HD, 2 * CMP_HID)), full((2 * CMP_HID, 2 * HD))],
        out_specs=pl.BlockSpec((Q_BLK, HEADS * HD), lambda b, i: (b * nq + i, 0)),
        out_shape=jax.ShapeDtypeStruct((n, HEADS * HD), BF16),
        scratch_shapes=[pltpu.VMEM((n_cmp, 2 * HD), F32), pltpu.VMEM((n_cmp, t_len), BF16),
                        pltpu.VMEM((Q_BLK, t_len), F32)],
        compiler_params=_cparams(2), name="nsa_prompt",
    )(q3, nsa_g, rows_nsa, kv7, kv7, kv7, kv7, pw['cmp_pos'], pw['w1bd'], pw['w2bd'])


def _merge_out_body(x_ref, g1_ref, wmg_ref, od_ref, on_ref, om_ref, wbr_ref, wo_ref, x1_ref):
    x = x_ref[...]
    d = x.shape[1]
    h = (x * lax.rsqrt(jnp.mean(x * x, axis=-1, keepdims=True) + EPS) * g1_ref[...]).astype(BF16)
    gate = jax.nn.sigmoid(_dot(h, wmg_ref[...]))
    mix = (gate[:, 0:d] * _dot(od_ref[...], wbr_ref[0]) + gate[:, d:2 * d] * _dot(on_ref[...], wbr_ref[1])
           + gate[:, 2 * d:3 * d] * _dot(om_ref[...], wbr_ref[2]))
    x1_ref[...] = x + _dot(mix.astype(BF16), wo_ref[...])


def _merge_out(x2d, o_dsa, o_nsa, o_mla, pw, tm):
    n, d = x2d.shape
    row = lambda w: pl.BlockSpec((tm, w), lambda i: (i, 0))
    full = lambda shape: pl.BlockSpec(shape, lambda i: (0,) * len(shape))
    return pl.pallas_call(
        _merge_out_body, grid=(n // tm,),
        in_specs=[row(d), full((1, d)), full((d, 3 * d)), row(HEADS * HD), row(HEADS * HD), row(HEADS * D_V),
                  full((3, HEADS * HD, d)), full((d, d))],
        out_specs=row(d), out_shape=jax.ShapeDtypeStruct((n, d), F32),
        compiler_params=_cparams(1), name="merge_out",
    )(x2d, pw['g1'], pw['wmg'], o_dsa, o_nsa, o_mla, pw['wbr'], pw['wo'])


def _ffn_body(x_ref, g2_ref, wg_ref, wu_ref, wd_ref, gf_ref, x2_ref, y_ref):
    x = x_ref[...]
    h = (x * lax.rsqrt(jnp.mean(x * x, axis=-1, keepdims=True) + EPS) * g2_ref[...]).astype(BF16)
    act = (jax.nn.silu(_dot(h, wg_ref[...])) * _dot(h, wu_ref[...])).astype(BF16)
    x2 = x + _dot(act, wd_ref[...])
    x2_ref[...] = x2
    y_ref[...] = x2 * lax.rsqrt(jnp.mean(x2 * x2, axis=-1, keepdims=True) + EPS) * gf_ref[...]


def _ffn(x1, pw, gf, tm):
    n, d = x1.shape
    dff = pw['wg'].shape[1]
    row = pl.BlockSpec((tm, d), lambda i: (i, 0))
    full = lambda shape: pl.BlockSpec(shape, lambda i: (0,) * len(shape))
    return pl.pallas_call(
        _ffn_body, grid=(n // tm,),
        in_specs=[row, full((1, d)), full((d, dff)), full((d, dff)), full((dff, d)), full((1, d))],
        out_specs=(row, row),
        out_shape=(jax.ShapeDtypeStruct((n, d), F32), jax.ShapeDtypeStruct((n, d), F32)),
        compiler_params=_cparams(1), name="ffn",
    )(x1, pw['g2'], pw['wg'], pw['wu'], pw['wd'], gf)


def _page_copy(pool_ref, layer, page, lo, width, buf_ref, slot, j, sem_ref):
    return pltpu.make_async_copy(pool_ref.at[layer, page, :, pl.ds(lo, width)],
                                 buf_ref.at[slot, pl.ds(j * PAGE_SIZE, PAGE_SIZE), :], sem_ref.at[slot])


def _stream_chunk(pt_ref, pool_ref, buf_ref, sem_ref, *, layer, lo, width, ppc):
    b, c = pl.program_id(0), pl.program_id(1)
    nb, nc = pl.num_programs(0), pl.num_programs(1)
    step = b * nc + c
    slot = step % 2

    def start(bb, cc, sl):
        def body(j, carry):
            _page_copy(pool_ref, layer, pt_ref[bb, cc * ppc + j], lo, width, buf_ref, sl, j, sem_ref).start()
            return carry
        lax.fori_loop(0, ppc, body, 0)

    @pl.when(step == 0)
    def _():
        start(b, c, slot)

    @pl.when(step + 1 < nb * nc)
    def _():
        last = c + 1 == nc
        start(jnp.where(last, b + 1, b), jnp.where(last, 0, c + 1), 1 - slot)

    def wait_body(j, carry):
        _page_copy(pool_ref, layer, 0, lo, width, buf_ref, slot, j, sem_ref).wait()
        return carry
    lax.fori_loop(0, ppc, wait_body, 0)
    return slot


def _pages_per_chunk(n_pages):
    ppc = min(32, n_pages)
    assert n_pages % ppc == 0
    return ppc


def _stream_call(body, pt, in_arrays, in_specs, pool, out_shape, out_specs, buf_width, scratch, name):
    bsz, n_pages = pt.shape
    ppc = _pages_per_chunk(n_pages)
    grid_spec = pltpu.PrefetchScalarGridSpec(
        num_scalar_prefetch=1, grid=(bsz, n_pages // ppc),
        in_specs=list(in_specs) + [pl.BlockSpec(memory_space=pl.ANY)],
        out_specs=out_specs,
        scratch_shapes=[pltpu.VMEM((2, ppc * PAGE_SIZE, buf_width), F32), pltpu.SemaphoreType.DMA((2,))]
        + list(scratch))
    return pl.pallas_call(body, grid_spec=grid_spec, out_shape=out_shape, compiler_params=_cparams(2),
                          name=name)(pt, *in_arrays, pool)


def _dsa_scores_body(pt_ref, iq_ref, iw_ref, newk_ref, pool_ref, sc_ref, snew_ref, buf_ref, sem_ref, *, layer, ppc):
    slot = _stream_chunk(pt_ref, pool_ref, buf_ref, sem_ref, layer=layer, lo=2 * HD, width=IDX_DIM, ppc=ppc)
    iq = iq_ref[...]
    w = iw_ref[...] * (IDX_DIM ** -0.5)
    s = jnp.maximum(_nt_dot(iq, buf_ref[slot].astype(BF16)), 0.0) * w
    sc_ref[...] = jnp.sum(s, axis=0, keepdims=True)

    @pl.when(pl.program_id(1) == pl.num_programs(1) - 1)
    def _():
        kn = newk_ref[...].astype(BF16).astype(F32)
        sn = jnp.maximum(jnp.sum(iq.astype(F32) * kn, axis=1, keepdims=True), 0.0) * w
        sn = jnp.sum(sn, axis=0, keepdims=True)
        lane = lax.broadcasted_iota(jnp.int32, (1, LANES), 1)
        snew_ref[...] = jnp.where(lane == 0, sn, NEG)


def _dsa_scores(pt, idx_q_s, idx_w_s, idx_k_new, cache, layer):
    bsz, n_pages = pt.shape
    ppc = _pages_per_chunk(n_pages)
    ch = ppc * PAGE_SIZE
    return _stream_call(
        functools.partial(_dsa_scores_body, layer=layer, ppc=ppc), pt,
        (idx_q_s, idx_w_s, idx_k_new),
        [pl.BlockSpec((None, HEADS, IDX_DIM), lambda b, c, pt: (b, 0, 0)),
         pl.BlockSpec((None, HEADS, 1), lambda b, c, pt: (b, 0, 0)),
         pl.BlockSpec((None, 1, IDX_DIM), lambda b, c, pt: (b, 0, 0))],
        cache,
        (jax.ShapeDtypeStruct((bsz, 1, n_pages * PAGE_SIZE), F32), jax.ShapeDtypeStruct((bsz, 1, LANES), F32)),
        (pl.BlockSpec((None, 1, ch), lambda b, c, pt: (b, 0, c)),
         pl.BlockSpec((None, 1, LANES), lambda b, c, pt: (b, 0, 0))),
        IDX_DIM, [], "dsa_scores")


def _dsa_thresh_body(sc_ref, snew_ref, thr_ref, quota_ref, key_ref, *, topk):
    past = sc_ref.shape[1]
    key_ref[:, 0:past] = _sort_key(sc_ref[...])
    key_ref[:, past:past + LANES] = _sort_key(snew_ref[...])
    thr = _kth_largest_key(key_ref, topk)
    key = key_ref[...]
    n_gt = _count(key > thr)
    n_eq = _count(key == thr)
    bits = thr ^ ((thr >> 31) & 0x7FFFFFFF)
    thr_ref[...] = jnp.broadcast_to(pltpu.bitcast(bits, F32), thr_ref.shape)
    quota = jnp.where(n_gt + n_eq > float(topk), float(topk) - n_gt, float(2 ** 24))
    quota_ref[...] = jnp.broadcast_to(quota, quota_ref.shape)


def _dsa_thresh(scores, snew, topk):
    bsz, past = scores.shape
    return pl.pallas_call(
        functools.partial(_dsa_thresh_body, topk=topk),
        out_shape=(jax.ShapeDtypeStruct((bsz, LANES), F32), jax.ShapeDtypeStruct((bsz, LANES), F32)),
        scratch_shapes=[pltpu.VMEM((bsz, past + LANES), jnp.int32)],
        compiler_params=pltpu.CompilerParams(vmem_limit_bytes=V7X_VMEM_LIMIT), name="dsa_thresh",
    )(scores, snew)


def _dsa_attend_body(pt_ref, q_ref, sc_ref, snew_ref, thr_ref, quota_ref, new_ref, pool_ref, o_ref,
                     buf_ref, sem_ref, m_ref, l_ref, acc_ref, used_ref, bias_ref, *, layer, ppc):
    slot = _stream_chunk(pt_ref, pool_ref, buf_ref, sem_ref, layer=layer, lo=0, width=2 * HD, ppc=ppc)
    c = pl.program_id(1)
    ch = ppc * PAGE_SIZE

    @pl.when(c == 0)
    def _():
        m_ref[...] = jnp.full(m_ref.shape, 2 * NEG, F32)
        l_ref[...] = jnp.zeros(l_ref.shape, F32)
        acc_ref[...] = jnp.zeros(acc_ref.shape, F32)
        used_ref[...] = jnp.zeros(used_ref.shape, F32)

    thr = thr_ref[:, 0:1]
    quota = quota_ref[:, 0:1]
    sc = sc_ref[...] + 0.0
    bias_ref[...] = jnp.where(sc >= thr, 0.0, NEG)

    @pl.when(jnp.max(quota_ref[...]) < float(2 ** 23))
    def _():
        pw = 512
        upper = _prefix_upper(pw)
        used = used_ref[...]
        for i in range(ch // pw):
            sci = sc[:, i * pw:(i + 1) * pw]
            eq = sci == thr
            e8 = jnp.broadcast_to(jnp.where(eq, 1.0, 0.0), (8, pw)).astype(BF16)
            pre = _dot(e8, upper)[0:1] + used
            bias_ref[:, i * pw:(i + 1) * pw] = jnp.where((sci > thr) | (eq & (pre <= quota)), 0.0, NEG)
            used = pre[:, pw - 1:pw]
        used_ref[...] = used

    q = q_ref[...]
    rows = buf_ref[slot].astype(BF16)
    s = _nt_dot(q, rows[:, :HD]) * (HD ** -0.5) + bias_ref[...]
    m, l, acc = _online_step(s, rows[:, HD:], m_ref[...], l_ref[...], acc_ref[...])
    m_ref[...], l_ref[...], acc_ref[...] = m, l, acc

    @pl.when(c == pl.num_programs(1) - 1)
    def _():
        new = new_ref[...].astype(BF16).astype(F32)
        sn = snew_ref[:, 0:1] + 0.0
        sel_n = (sn > thr) | ((sn == thr) & (used_ref[...] + 1.0 <= quota))
        s_n = jnp.where(sel_n, jnp.sum(q.astype(F32) * new[:, :HD], axis=1, keepdims=True) * (HD ** -0.5), NEG)
        m_new = jnp.maximum(m, s_n)
        alpha = jnp.exp(m - m_new)
        p_n = jnp.exp(s_n - m_new).astype(BF16).astype(F32)
        o = (alpha * acc + p_n * new[:, HD:2 * HD]) / (alpha * l + p_n)
        o_ref[...] = o.astype(BF16)


def _dsa_attend(pt, dsa_q_s, scores, snew, thr, quota, rows_new, cache, layer):
    bsz, n_pages = pt.shape
    ppc = _pages_per_chunk(n_pages)
    ch = ppc * PAGE_SIZE
    per_b = lambda w: pl.BlockSpec((None, 1, w), lambda b, c, pt: (b, 0, 0))
    return _stream_call(
        functools.partial(_dsa_attend_body, layer=layer, ppc=ppc), pt,
        (dsa_q_s, scores, snew, thr, quota, rows_new),
        [pl.BlockSpec((None, HEADS, HD), lambda b, c, pt: (b, 0, 0)),
         pl.BlockSpec((None, 1, ch), lambda b, c, pt: (b, 0, c)),
         per_b(LANES), per_b(LANES), per_b(LANES), per_b(DSA_ROW)],
        cache,
        jax.ShapeDtypeStruct((bsz, HEADS, HD), BF16),
        pl.BlockSpec((None, HEADS, HD), lambda b, c, pt: (b, 0, 0)),
        2 * HD,
        [pltpu.VMEM((HEADS, 1), F32), pltpu.VMEM((HEADS, 1), F32), pltpu.VMEM((HEADS, HD), F32),
         pltpu.VMEM((1, 1), F32), pltpu.VMEM((1, ch), F32)],
        "dsa_attend")


def _mla_decode_body(pt_ref, q_ref, new_ref, wuv_ref, pool_ref, o_ref, buf_ref, sem_ref, m_ref, l_ref, acc_ref,
                     *, layer, ppc):
    slot = _stream_chunk(pt_ref, pool_ref, buf_ref, sem_ref, layer=layer, lo=0, width=MLA_ROW, ppc=ppc)
    c = pl.program_id(1)
    scale = (D_NOPE + D_ROPE) ** -0.5

    @pl.when(c == 0)
    def _():
        m_ref[...] = jnp.full(m_ref.shape, 2 * NEG, F32)
        l_ref[...] = jnp.zeros(l_ref.shape, F32)
        acc_ref[...] = jnp.zeros(acc_ref.shape, F32)

    q = q_ref[...]
    rows = buf_ref[slot].astype(BF16)
    s = _nt_dot(q, rows) * scale
    m, l, acc = _online_step(s, rows[:, :D_C], m_ref[...], l_ref[...], acc_ref[...])
    m_ref[...], l_ref[...], acc_ref[...] = m, l, acc

    @pl.when(c == pl.num_programs(1) - 1)
    def _():
        new = new_ref[...].astype(BF16).astype(F32)
        s_n = jnp.sum(q.astype(F32) * new, axis=1, keepdims=True) * scale
        m_new = jnp.maximum(m, s_n)
        alpha = jnp.exp(m - m_new)
        p_n = jnp.exp(s_n - m_new).astype(BF16).astype(F32)
        lat = ((alpha * acc + p_n * new[:, :D_C]) / (alpha * l + p_n)).astype(BF16)
        full = _dot(lat, wuv_ref[...])
        hrow = lax.broadcasted_iota(jnp.int32, full.shape, 0)
        hcol = lax.broadcasted_iota(jnp.int32, full.shape, 1) // D_V
        o_ref[...] = jnp.sum(jnp.where(hrow == hcol, full, 0.0), axis=0, keepdims=True).astype(BF16)


def _mla_decode(pt, qcat_s, rows_new, wuv_flat, cache, layer):
    bsz, n_pages = pt.shape
    ppc = _pages_per_chunk(n_pages)
    return _stream_call(
        functools.partial(_mla_decode_body, layer=layer, ppc=ppc), pt,
        (qcat_s, rows_new, wuv_flat),
        [pl.BlockSpec((None, HEADS, MLA_ROW), lambda b, c, pt: (b, 0, 0)),
         pl.BlockSpec((None, 1, MLA_ROW), lambda b, c, pt: (b, 0, 0)),
         pl.BlockSpec((D_C, HEADS * D_V), lambda b, c, pt: (0, 0))],
        cache,
        jax.ShapeDtypeStruct((bsz, 1, HEADS * D_V), BF16),
        pl.BlockSpec((None, 1, HEADS * D_V), lambda b, c, pt: (b, 0, 0)),
        MLA_ROW,
        [pltpu.VMEM((HEADS, 1), F32), pltpu.VMEM((HEADS, 1), F32), pltpu.VMEM((HEADS, D_C), F32)],
        "mla_decode")


def _nsa_cmp_body(pt_ref, q_ref, pos_ref, w1_ref, w2_ref, pool_ref, oc_ref, imp_ref, buf_ref, sem_ref, ckv_ref,
                  *, layer, ppc):
    slot = _stream_chunk(pt_ref, pool_ref, buf_ref, sem_ref, layer=layer, lo=0, width=2 * HD, ppc=ppc)
    c = pl.program_id(1)
    n_blk = ppc * PAGE_SIZE // CMP_BLK
    out = _compress_blocks(lambda r: buf_ref[slot, pl.ds(r, n_blk, stride=CMP_BLK), :], pos_ref, w1_ref, w2_ref,
                           n_blk)
    ckv_ref[pl.ds(pl.multiple_of(c * n_blk, n_blk), n_blk), :] = out

    @pl.when(c == pl.num_programs(1) - 1)
    def _():
        n_cmp = ckv_ref.shape[0]
        vis = jnp.full((HEADS, n_cmp), True)
        o_c, p = _cmp_attend(q_ref[...], ckv_ref[...], vis)
        oc_ref[...] = o_c
        imp = jnp.sum(p, axis=0, keepdims=True)
        imp_ref[...] = imp + pltpu.roll(imp, n_cmp - 1, 1)


def _nsa_cmp(pt, nsa_q_s, pw, cache, layer):
    bsz, n_pages = pt.shape
    ppc = _pages_per_chunk(n_pages)
    n_cmp = n_pages * PAGE_SIZE // CMP_BLK
    full = lambda shape: pl.BlockSpec(shape, lambda b, c, pt: (0,) * len(shape))
    return _stream_call(
        functools.partial(_nsa_cmp_body, layer=layer, ppc=ppc), pt,
        (nsa_q_s, pw['cmp_pos'], pw['w1bd'], pw['w2bd']),
        [pl.BlockSpec((None, HEADS, HD), lambda b, c, pt: (b, 0, 0)),
         full((CMP_BLK, 2 * HD)), full((CMP_BLK, 2 * HD, 2 * CMP_HID)), full((2 * CMP_HID, 2 * HD))],
        cache,
        (jax.ShapeDtypeStruct((bsz, HEADS, HD), F32), jax.ShapeDtypeStruct((bsz, 1, n_cmp), F32)),
        (pl.BlockSpec((None, HEADS, HD), lambda b, c, pt: (b, 0, 0)),
         pl.BlockSpec((None, 1, n_cmp), lambda b, c, pt: (b, 0, 0))),
        2 * HD, [pltpu.VMEM((n_cmp, 2 * HD), F32)], "nsa_cmp")


def _nsa_pick_body(imp_ref, ids_ref, *, n_pick):
    imp = imp_ref[...]
    width = imp.shape[1]
    ratio = SEL_BLK // CMP_BLK
    lane = lax.broadcasted_iota(jnp.int32, imp.shape, 1)
    val = jnp.where((lane % ratio == 0) & (lane >= ratio), imp, -jnp.inf)
    slot = lax.broadcasted_iota(jnp.int32, ids_ref.shape, 1)
    ids = jnp.zeros(ids_ref.shape, jnp.int32)
    for i in range(n_pick):
        m = jnp.max(val, axis=1, keepdims=True)
        idx = jnp.min(jnp.where(val == m, lane, width), axis=1, keepdims=True)
        ids = jnp.where(slot == i + 1, idx // ratio, ids)
        val = jnp.where(lane == idx, -jnp.inf, val)
    ids_ref[...] = ids


def _nsa_pick(imp, n_pick):
    bsz = imp.shape[0]
    return pl.pallas_call(
        functools.partial(_nsa_pick_body, n_pick=n_pick),
        out_shape=jax.ShapeDtypeStruct((bsz, LANES), jnp.int32), name="nsa_pick",
    )(imp)


def _nsa_sel_body(pt_ref, ids_ref, q_ref, g_ref, oc_ref, new_ref, newwin_ref, win_ref, pool_ref,
                  o_ref, wout_ref, buf_ref, sem_ref, *, layer, n_blk):
    b = pl.program_id(0)
    nb = pl.num_programs(0)
    slot = b % 2
    per_page = PAGE_SIZE // SEL_BLK

    def copy(bb, i, sl):
        blk = ids_ref[bb, i]
        page = pt_ref[bb, blk // per_page]
        return pltpu.make_async_copy(
            pool_ref.at[layer, page, pl.ds((blk % per_page) * SEL_BLK, SEL_BLK), pl.ds(2 * HD, 2 * HD)],
            buf_ref.at[sl, pl.ds(i * SEL_BLK, SEL_BLK), :], sem_ref.at[sl])

    def start(bb, sl):
        for i in range(n_blk):
            copy(bb, i, sl).start()

    @pl.when(b == 0)
    def _():
        start(b, slot)

    @pl.when(b + 1 < nb)
    def _():
        start(b + 1, 1 - slot)

    for i in range(n_blk):
        copy(b, i, slot).wait()

    q = q_ref[...]
    qf = q.astype(F32)
    scale = HD ** -0.5

    def attend(rows, new):
        rows = rows.astype(BF16)
        new = new.astype(BF16).astype(F32)
        s = _nt_dot(q, rows[:, :HD]) * scale
        s_n = jnp.sum(qf * new[:, :HD], axis=1, keepdims=True) * scale
        m = jnp.maximum(jnp.max(s, axis=1, keepdims=True), s_n)
        e = jnp.exp(s - m)
        e_n = jnp.exp(s_n - m)
        num = _dot(e.astype(BF16), rows[:, HD:]) + e_n.astype(BF16).astype(F32) * new[:, HD:]
        return num / (jnp.sum(e, axis=1, keepdims=True) + e_n)

    new = new_ref[...]
    o_s = attend(buf_ref[slot], new[:, 2 * HD:])
    win = win_ref[...]
    newwin = newwin_ref[...]
    o_w = attend(win, newwin)
    g = g_ref[...]
    o_ref[...] = (g[:, 0:1] * oc_ref[...] + g[:, 1:2] * o_s + g[:, 2:3] * o_w).astype(BF16)
    w = win.shape[0]
    wout_ref[0:w - 1, :] = win[1:w, :]
    wout_ref[w - 1:w, :] = newwin


def _nsa_sel(pt, ids, nsa_q_s, gates_s, o_c, rows_new, win_new, win_buf, cache, layer, n_blk):
    bsz = pt.shape[0]
    w = win_buf.shape[1]
    per_b = lambda r, c: pl.BlockSpec((None, r, c), lambda b, pt, ids: (b, 0, 0))
    grid_spec = pltpu.PrefetchScalarGridSpec(
        num_scalar_prefetch=2, grid=(bsz,),
        in_specs=[per_b(HEADS, HD), per_b(HEADS, 3), per_b(HEADS, HD), per_b(1, NSA_ROW), per_b(1, WIN_ROW),
                  per_b(w, WIN_ROW), pl.BlockSpec(memory_space=pl.ANY)],
        out_specs=(per_b(HEADS, HD), per_b(w, WIN_ROW)),
        scratch_shapes=[pltpu.VMEM((2, n_blk * SEL_BLK, 2 * HD), F32), pltpu.SemaphoreType.DMA((2,))])
    return pl.pallas_call(
        functools.partial(_nsa_sel_body, layer=layer, n_blk=n_blk), grid_spec=grid_spec,
        out_shape=(jax.ShapeDtypeStruct((bsz, HEADS, HD), BF16), jax.ShapeDtypeStruct((bsz, w, WIN_ROW), F32)),
        compiler_params=_cparams(1), name="nsa_sel",
    )(pt, ids, nsa_q_s, gates_s, o_c, rows_new, win_new, win_buf, cache)


def _layer_prompt(x2d, pw, gf, tabs64, tabs32, bsz, t_len):
    tm = 256
    (q3, qcat, idx_w, nsa_g, rows_dsa, rows_nsa, rows_mla, win, kv7, mla_bf) = _in_proj(
        x2d, pw, tabs64, tabs32, tm, t_len // tm)
    o_dsa = _dsa_prompt(q3, idx_w, kv7, bsz, t_len)
    o_nsa = _nsa_prompt(q3, nsa_g, rows_nsa, kv7, pw, bsz, t_len)
    o_mla = _mla_prompt(qcat, mla_bf, pw['wuv'], bsz, t_len)
    x1 = _merge_out(x2d, o_dsa, o_nsa, o_mla, pw, tm)
    x2, y = _ffn(x1, pw, gf, tm)
    return x2, y, rows_dsa, rows_nsa, rows_mla, win


def _layer_sample(x2d, pw, gf, tabs64, tabs32, layer, cache_dsa, cache_nsa, cache_mla, win_buf, pt):
    bsz = x2d.shape[0]
    n_pages = pt.shape[1]
    past = n_pages * PAGE_SIZE
    (q3, qcat, idx_w, nsa_g, rows_dsa, rows_nsa, rows_mla, win, _, _) = _in_proj(x2d, pw, tabs64, tabs32, bsz, 1)
    heads_first = lambda a: jnp.transpose(a, (1, 0, 2))
    dsa_q, idx_q, nsa_q = heads_first(q3[0]), heads_first(q3[1]), heads_first(q3[2])
    qcat_s = heads_first(qcat)
    scores, snew = _dsa_scores(pt, idx_q, idx_w[:, :, None], rows_dsa[:, None, 2 * HD:], cache_dsa, layer)
    topk = min(DSA_TOPK, (past + 1) // 4)
    thr, quota = _dsa_thresh(scores[:, 0, :], snew[:, 0, :], topk)
    o_dsa = _dsa_attend(pt, dsa_q, scores, snew, thr[:, None, :], quota[:, None, :], rows_dsa[:, None, :],
                        cache_dsa, layer)
    n_past_blk = past // SEL_BLK
    n_pick = min(N_SEL, n_past_blk + 1) - 2
    assert past % SEL_BLK == 0 and 0 <= n_pick <= n_past_blk - 1
    o_c, imp = _nsa_cmp(pt, nsa_q, pw, cache_nsa, layer)
    ids = _nsa_pick(imp[:, 0, :], n_pick)[:, :n_pick + 1]
    gates = jnp.transpose(nsa_g.reshape(bsz, 3, HEADS), (0, 2, 1))
    o_nsa, win_out = _nsa_sel(pt, ids, nsa_q, gates, o_c, rows_nsa[:, None, :], win[:, None, :], win_buf,
                              cache_nsa, layer, n_pick + 1)
    wuv_flat = jnp.transpose(pw['wuv'], (1, 0, 2)).reshape(D_C, HEADS * D_V)
    o_mla = _mla_decode(pt, qcat_s, rows_mla[:, None, :], wuv_flat, cache_mla, layer)
    x1 = _merge_out(x2d, o_dsa.reshape(bsz, HEADS * HD), o_nsa.reshape(bsz, HEADS * HD),
                    o_mla.reshape(bsz, HEADS * D_V), pw, bsz)
    x2, y = _ffn(x1, pw, gf, bsz)
    return x2, y, rows_dsa, rows_nsa, rows_mla, win_out


def kernel(x_prompt, x_sample, cache_dsa, cache_nsa, cache_mla, state_nsa_win, page_table, norm1_g, w_in,
           mla_q_norm_g, w_uq, mla_kv_norm_g, w_uk, w_uv, cmp_pos, cmp_w1, cmp_w2, w_br_dsa, w_br_nsa, w_br_mla,
           w_o, norm2_g, w_ff_gate, w_ff_up, w_ff_down, final_norm_g):
    bsz, t_len, d = x_prompt.shape
    dec_b, dec_t, _ = x_sample.shape
    depth = w_in.shape[0]
    past = page_table.shape[1] * PAGE_SIZE
    assert dec_t == 1 and t_len % KV_CHUNK == 0 and state_nsa_win.shape[2] == min(WINDOW, past)
    pos_p = jnp.arange(t_len, dtype=jnp.int32)
    pos_s = jnp.full((dec_b,), past, dtype=jnp.int32)
    tabs_p = (_rope_tables(pos_p, HD), _rope_tables(pos_p, D_ROPE))
    tabs_s = (_rope_tables(pos_s, HD), _rope_tables(pos_s, D_ROPE))
    gf = final_norm_g[None, :]
    xp = x_prompt.reshape(bsz * t_len, d)
    xs = x_sample.reshape(dec_b, d)
    outs = [[] for _ in range(8)]
    yp = ys = None
    for l in range(depth):
        lw = dict(norm1_g=norm1_g[l], w_in=w_in[l], mla_q_norm_g=mla_q_norm_g[l], w_uq=w_uq[l],
                  mla_kv_norm_g=mla_kv_norm_g[l], w_uk=w_uk[l], w_uv=w_uv[l], cmp_pos=cmp_pos[l],
                  cmp_w1=cmp_w1[l], cmp_w2=cmp_w2[l], w_br_dsa=w_br_dsa[l], w_br_nsa=w_br_nsa[l],
                  w_br_mla=w_br_mla[l], w_o=w_o[l], norm2_g=norm2_g[l], w_ff_gate=w_ff_gate[l],
                  w_ff_up=w_ff_up[l], w_ff_down=w_ff_down[l])
        pw = _prep_layer(lw)
        xp, yp, r_dsa, r_nsa, r_mla, r_win = _layer_prompt(xp, pw, gf, *tabs_p, bsz, t_len)
        wlen = min(WINDOW, t_len)
        outs[0].append(r_dsa.reshape(bsz, t_len, DSA_ROW))
        outs[2].append(r_nsa.reshape(bsz, t_len, NSA_ROW))
        outs[4].append(r_mla.reshape(bsz, t_len, MLA_ROW))
        outs[6].append(r_win.reshape(bsz, t_len, WIN_ROW)[:, t_len - wlen:])
        xs, ys, r_dsa, r_nsa, r_mla, r_win = _layer_sample(xs, pw, gf, *tabs_s, l, cache_dsa, cache_nsa, cache_mla,
                                                           state_nsa_win[l], page_table)
        outs[1].append(r_dsa.reshape(dec_b, 1, DSA_ROW))
        outs[3].append(r_nsa.reshape(dec_b, 1, NSA_ROW))
        outs[5].append(r_mla.reshape(dec_b, 1, MLA_ROW))
        outs[7].append(r_win)
    return (yp.reshape(bsz, t_len, d), ys.reshape(dec_b, 1, d), *(jnp.stack(o) for o in outs))
```

```python
import functools
import struct

import jax
import jax.numpy as jnp
from jax import lax
from jax.experimental import pallas as pl
from jax.experimental.pallas import tpu as pltpu

F32 = jnp.float32
BF16 = jnp.bfloat16

HD = 64
HEADS = 8
IDX_DIM = 64
DSA_TOPK = 256
CMP_BLK = 32
CMP_HID = 2 * HD
SEL_BLK = 64
N_SEL = 16
WINDOW = 512
D_CQ = 256
D_C = 128
D_NOPE = 64
D_ROPE = 32
D_V = 64
ROPE_THETA = 10000.0
EPS = 1e-6
NEG = -1e30
FORCE = 1e9
PAGE_SIZE = 128

DSA_ROW = 2 * HD + IDX_DIM
NSA_ROW = 4 * HD
MLA_ROW = D_C + D_ROPE
WIN_ROW = 2 * HD

IN_SPLITS = (
    ('dsa_q', HEADS * HD), ('dsa_k', HD), ('dsa_v', HD),
    ('idx_q', HEADS * IDX_DIM), ('idx_k', IDX_DIM), ('idx_w', HEADS),
    ('nsa_q', HEADS * HD), ('cmp_k', HD), ('cmp_v', HD), ('sel_k', HD), ('sel_v', HD),
    ('win_k', HD), ('win_v', HD), ('nsa_g', 3 * HEADS),
    ('mla_cq', D_CQ), ('mla_ckv', D_C), ('mla_kr', D_ROPE),
)

LANES = 128
V7X_VMEM_LIMIT = 56 * 1024 * 1024

Q_BLK = 128
DSA_Q_BLK = 256
KV_CHUNK = 512
ROPE_W = 15 * LANES
PLAIN_W = 6 * LANES


def _f32_key(v):
    bits = struct.unpack('<i', struct.pack('<f', v))[0]
    return bits ^ ((bits >> 31) & 0x7FFFFFFF)


NEG_KEY = _f32_key(NEG)
INT_MIN = -2 ** 31


def _sort_key(x):
    bits = pltpu.bitcast(x + 0.0, jnp.int32)
    return bits ^ ((bits >> 31) & 0x7FFFFFFF)


def _nt_dot(a, b):
    return lax.dot_general(a, b, (((1,), (1,)), ((), ())), preferred_element_type=F32)


def _dot(a, b):
    return jnp.dot(a, b, preferred_element_type=F32)


def _rope128(x, c, sa, sb, half):
    return x * c + pltpu.roll(x, LANES - half, 1) * sa + pltpu.roll(x, half, 1) * sb


def _cparams(n_axes):
    return pltpu.CompilerParams(dimension_semantics=("arbitrary",) * n_axes, vmem_limit_bytes=V7X_VMEM_LIMIT)


def _in_proj_body(x_ref, g1_ref, wr_ref, wp_ref, c64_ref, sa64_ref, sb64_ref, c32_ref, sa32_ref, sb32_ref,
                  gq_ref, wuq_ref, wuk_ref, gkv_ref,
                  q3_ref, qcat_ref, idxw_ref, nsag_ref, rdsa_ref, rnsa_ref, rmla_ref, win_ref, kv7_ref, mlabf_ref):
    x = x_ref[...]
    h = (x * lax.rsqrt(jnp.mean(x * x, axis=-1, keepdims=True) + EPS) * g1_ref[...]).astype(BF16)
    a = _dot(h, wr_ref[...])
    b = _dot(h, wp_ref[...])
    c64, sa64, sb64 = c64_ref[...], sa64_ref[...], sb64_ref[...]
    c32, sa32, sb32 = c32_ref[...], sa32_ref[...], sb32_ref[...]
    rg = [_rope128(a[:, g * LANES:(g + 1) * LANES], c64, sa64, sb64, HD // 2) for g in range(ROPE_W // LANES)]
    for grp, qscale in enumerate((HD ** -0.5, 1.0, HD ** -0.5)):
        for hh in range(HEADS):
            g, o = grp * 4 + hh // 2, (hh % 2) * HD
            q3_ref[grp, hh] = (rg[g][:, o:o + HD] * qscale).astype(BF16)
    dsa_k, idx_k = rg[12][:, :HD], rg[12][:, HD:]
    cmp_k, sel_k = rg[13][:, :HD], rg[13][:, HD:]
    win_k = rg[14][:, :HD]
    dsa_v, cmp_v, sel_v, win_v = (b[:, i * HD:(i + 1) * HD] for i in range(4))
    rdsa_ref[:, 0:HD] = dsa_k
    rdsa_ref[:, HD:2 * HD] = dsa_v
    rdsa_ref[:, 2 * HD:3 * HD] = idx_k
    rnsa_ref[:, 0:HD] = cmp_k
    rnsa_ref[:, HD:2 * HD] = cmp_v
    rnsa_ref[:, 2 * HD:3 * HD] = sel_k
    rnsa_ref[:, 3 * HD:4 * HD] = sel_v
    win_ref[:, 0:HD] = win_k
    win_ref[:, HD:2 * HD] = win_v
    for i, t in enumerate((dsa_k, dsa_v, idx_k, sel_k, sel_v, win_k, win_v)):
        kv7_ref[i] = t.astype(BF16)
    ckv = b[:, 2 * LANES:3 * LANES]
    ckv = ckv * lax.rsqrt(jnp.mean(ckv * ckv, axis=-1, keepdims=True) + EPS) * gkv_ref[...]
    tail = b[:, 5 * LANES:6 * LANES]
    kr = _rope128(tail, c32, sa32, sb32, D_ROPE // 2)[:, :D_ROPE]
    rmla_ref[:, 0:D_C] = ckv
    rmla_ref[:, D_C:MLA_ROW] = kr
    mlabf_ref[:, 0:D_C] = ckv.astype(BF16)
    mlabf_ref[:, D_C:MLA_ROW] = kr.astype(BF16)
    idxw_ref[...] = tail[:, D_ROPE:D_ROPE + HEADS] * (HEADS ** -0.5)
    nsag_ref[...] = jax.nn.sigmoid(tail[:, D_ROPE + HEADS:D_ROPE + 4 * HEADS])
    cq = b[:, 3 * LANES:5 * LANES]
    cq = (cq * lax.rsqrt(jnp.mean(cq * cq, axis=-1, keepdims=True) + EPS) * gq_ref[...]).astype(BF16)
    qm = _dot(cq, wuq_ref[...])
    qlat = _dot(qm[:, :HEADS * D_NOPE].astype(BF16), wuk_ref[...])
    qr = [_rope128(qm[:, HEADS * D_NOPE + g * LANES:HEADS * D_NOPE + (g + 1) * LANES], c32, sa32, sb32, D_ROPE // 2)
          for g in range(HEADS * D_ROPE // LANES)]
    per = LANES // D_ROPE
    for hh in range(HEADS):
        qcat_ref[hh, :, 0:D_C] = qlat[:, hh * D_C:(hh + 1) * D_C].astype(BF16)
        o = (hh % per) * D_ROPE
        qcat_ref[hh, :, D_C:MLA_ROW] = qr[hh // per][:, o:o + D_ROPE].astype(BF16)


def _rope_tables(pos, d):
    half = d // 2
    inv = ROPE_THETA ** (-jnp.arange(half, dtype=F32) * 2.0 / d)
    ang = pos.astype(F32)[:, None] * inv[None, :]
    cos, sin = jnp.cos(ang), jnp.sin(ang)
    z = jnp.zeros_like(sin)
    rep = LANES // d
    return (jnp.tile(jnp.concatenate([cos, cos], 1), (1, rep)),
            jnp.tile(jnp.concatenate([-sin, z], 1), (1, rep)),
            jnp.tile(jnp.concatenate([z, sin], 1), (1, rep)))


def _split_w_in(w_in_l):
    cols, off = {}, 0
    for name, w in IN_SPLITS:
        cols[name] = w_in_l[:, off:off + w]
        off += w
    cols['merge_g'] = w_in_l[:, off:]
    return cols


def _prep_layer(lw):
    c = _split_w_in(lw['w_in'])
    d = lw['w_in'].shape[0]
    z64 = jnp.zeros((d, HD), F32)
    wr = jnp.concatenate([c['dsa_q'], c['idx_q'], c['nsa_q'], c['dsa_k'], c['idx_k'], c['cmp_k'], c['sel_k'],
                          c['win_k'], z64], axis=1)
    wp = jnp.concatenate([c['dsa_v'], c['cmp_v'], c['sel_v'], c['win_v'], c['mla_ckv'], c['mla_cq'],
                          c['mla_kr'], c['idx_w'], c['nsa_g'], z64], axis=1)
    assert wr.shape[1] == ROPE_W and wp.shape[1] == PLAIN_W
    wuq = lw['w_uq'].reshape(D_CQ, HEADS, D_NOPE + D_ROPE)
    wuq = jnp.concatenate([wuq[:, :, :D_NOPE].reshape(D_CQ, HEADS * D_NOPE),
                           wuq[:, :, D_NOPE:].reshape(D_CQ, HEADS * D_ROPE)], axis=1)
    wk = jnp.transpose(lw['w_uk'], (1, 2, 0))
    wuk_bd = (jnp.eye(HEADS, dtype=F32)[:, None, :, None] * wk[:, :, None, :]).reshape(HEADS * D_NOPE, HEADS * D_C)
    wuv = jnp.transpose(lw['w_uv'], (1, 0, 2))
    w1 = lw['cmp_w1'].reshape(2, CMP_BLK, HD, CMP_HID)
    zz = jnp.zeros((CMP_BLK, HD, CMP_HID), F32)
    w1bd = jnp.concatenate([jnp.concatenate([w1[0], zz], axis=2), jnp.concatenate([zz, w1[1]], axis=2)], axis=1)
    z2 = jnp.zeros((CMP_HID, HD), F32)
    w2bd = jnp.concatenate([jnp.concatenate([lw['cmp_w2'][0], z2], axis=1),
                            jnp.concatenate([z2, lw['cmp_w2'][1]], axis=1)], axis=0)
    return dict(
        g1=lw['norm1_g'][None, :], wr=wr.astype(BF16), wp=wp.astype(BF16),
        gq=lw['mla_q_norm_g'][None, :], wuq=wuq.astype(BF16), wuk=wuk_bd.astype(BF16),
        gkv=lw['mla_kv_norm_g'][None, :], wuv=wuv.astype(BF16),
        cmp_pos=lw['cmp_pos'].reshape(CMP_BLK, 2 * HD), w1bd=w1bd.astype(BF16), w2bd=w2bd.astype(BF16),
        wmg=c['merge_g'].astype(BF16),
        wbr=jnp.stack([lw['w_br_dsa'], lw['w_br_nsa'], lw['w_br_mla']]).astype(BF16),
        wo=lw['w_o'].astype(BF16), g2=lw['norm2_g'][None, :],
        wg=lw['w_ff_gate'].astype(BF16), wu=lw['w_ff_up'].astype(BF16), wd=lw['w_ff_down'].astype(BF16),
    )


def _in_proj(x2d, pw, tabs64, tabs32, tm, n_pos_blocks):
    n, d = x2d.shape
    full = lambda shape: pl.BlockSpec(shape, lambda i: (0,) * len(shape))
    tab = pl.BlockSpec((tm, LANES), lambda i: (i % n_pos_blocks, 0))
    row = lambda w: pl.BlockSpec((tm, w), lambda i: (i, 0))
    out_shape = (
        jax.ShapeDtypeStruct((3, HEADS, n, HD), BF16),
        jax.ShapeDtypeStruct((HEADS, n, MLA_ROW), BF16),
        jax.ShapeDtypeStruct((n, HEADS), F32),
        jax.ShapeDtypeStruct((n, 3 * HEADS), F32),
        jax.ShapeDtypeStruct((n, DSA_ROW), F32),
        jax.ShapeDtypeStruct((n, NSA_ROW), F32),
        jax.ShapeDtypeStruct((n, MLA_ROW), F32),
        jax.ShapeDtypeStruct((n, WIN_ROW), F32),
        jax.ShapeDtypeStruct((7, n, HD), BF16),
        jax.ShapeDtypeStruct((n, MLA_ROW), BF16),
    )
    out_specs = (
        pl.BlockSpec((3, HEADS, tm, HD), lambda i: (0, 0, i, 0)),
        pl.BlockSpec((HEADS, tm, MLA_ROW), lambda i: (0, i, 0)),
        row(HEADS), row(3 * HEADS), row(DSA_ROW), row(NSA_ROW), row(MLA_ROW), row(WIN_ROW),
        pl.BlockSpec((7, tm, HD), lambda i: (0, i, 0)),
        row(MLA_ROW),
    )
    in_specs = [row(d), full((1, d)), full((d, ROPE_W)), full((d, PLAIN_W)), tab, tab, tab, tab, tab, tab,
                full((1, D_CQ)), full((D_CQ, HEADS * (D_NOPE + D_ROPE))), full((HEADS * D_NOPE, HEADS * D_C)),
                full((1, D_C))]
    return pl.pallas_call(
        _in_proj_body, grid=(n // tm,), in_specs=in_specs, out_specs=out_specs, out_shape=out_shape,
        compiler_params=_cparams(1), name="in_proj",
    )(x2d, pw['g1'], pw['wr'], pw['wp'], *tabs64, *tabs32, pw['gq'], pw['wuq'], pw['wuk'], pw['gkv'])


def _online_step(s, m, l):
    m_new = jnp.maximum(m, jnp.max(s, axis=1, keepdims=True))
    alpha = jnp.exp(m - m_new)
    p = jnp.exp(s - m_new)
    return m_new, alpha * l + jnp.sum(p, axis=1, keepdims=True), alpha, p.astype(BF16)


def _attn_init(m_rows, dv):
    return (jnp.full((m_rows, 1), 2 * NEG, F32), jnp.zeros((m_rows, 1), F32), jnp.zeros((m_rows, dv), F32))


def _attn_chunk(carry, q, k, v, bias, q_blk, scale=None):
    m, l, acc = carry
    width = k.shape[0]
    s = _nt_dot(q, k)
    if scale is not None:
        s = s * scale
    s = (s.reshape(HEADS, q_blk, width) + bias[None]).reshape(HEADS * q_blk, width)
    m, l, alpha, p = _online_step(s, m, l)
    return m, l, alpha * acc + _dot(p, v)


def _attn_result(carry):
    return carry[2] / carry[1]


def _heads_to_lanes(o, q):
    return jnp.concatenate([o[hh * q:(hh + 1) * q] for hh in range(HEADS)], axis=1)


def _count(mask):
    return jnp.sum(jnp.where(mask, 1.0, 0.0), axis=1, keepdims=True)


def _kth_largest_key(key_ref, k, width):
    rows = key_ref.shape[0]
    half = rows // 2
    kf = float(k)

    def part(r0, cand):
        acc = None
        for j in range(width // LANES):
            hit = jnp.where(key_ref[r0:r0 + half, j * LANES:(j + 1) * LANES] >= cand, 1.0, 0.0)
            acc = hit if acc is None else acc + hit
        return acc

    def settle(p, cand, base):
        return jnp.where(jnp.sum(p, axis=1, keepdims=True) >= kf, cand, base)

    zero = jnp.zeros((half, 1), jnp.int32)
    lowest = jnp.full((half, 1), INT_MIN, jnp.int32)
    base_a = settle(part(0, zero), zero, lowest)
    base_b = settle(part(half, zero), zero, lowest)

    def body(i, carry):
        base_a, base_b, part_b, cand_b = carry
        bit = lax.shift_left(jnp.int32(1), 30 - i)
        cand_a = base_a | bit
        part_a = part(0, cand_a)
        base_b = settle(part_b, cand_b, base_b)
        cand_b = base_b | bit
        part_b = part(half, cand_b)
        return settle(part_a, cand_a, base_a), base_b, part_b, cand_b

    base_a, base_b, part_b, cand_b = lax.fori_loop(
        0, 31, body, (base_a, base_b, jnp.zeros((half, LANES), F32), base_b))
    return jnp.concatenate([base_a, settle(part_b, cand_b, base_b)], axis=0)


def _prefix_upper(n):
    r = lax.broadcasted_iota(jnp.int32, (n, n), 0)
    c = lax.broadcasted_iota(jnp.int32, (n, n), 1)
    return jnp.where(r <= c, 1.0, 0.0).astype(BF16)


def _dsa_prompt_body(iq_ref, iw_ref, q_ref, ik_ref, k_ref, v_ref, o_ref, key_ref, bias_ref, thr_ref, *,
                     t_len, topk):
    q_blk = DSA_Q_BLK
    qi = pl.program_id(1)
    n_chunks = (qi * q_blk + q_blk + KV_CHUNK - 1) // KV_CHUNK
    t = qi * q_blk + lax.broadcasted_iota(jnp.int32, (q_blk, 1), 0)
    lane = lax.broadcasted_iota(jnp.int32, (1, KV_CHUNK), 1)
    iq = iq_ref[...].reshape(HEADS * q_blk, IDX_DIM)
    w = iw_ref[...] * (IDX_DIM ** -0.5)

    def idx_body(c, carry):
        ks = pl.multiple_of(c * KV_CHUNK, KV_CHUNK)
        s = jnp.maximum(_nt_dot(iq, ik_ref[pl.ds(ks, KV_CHUNK), :]), 0.0).reshape(HEADS, q_blk, KV_CHUNK)
        tot = w[:, 0:1] * s[0]
        for hh in range(1, HEADS):
            tot = tot + w[:, hh:hh + 1] * s[hh]
        key_ref[:, pl.ds(ks, KV_CHUNK)] = _sort_key(jnp.where(ks + lane <= t, tot, NEG))
        return carry

    lax.fori_loop(0, n_chunks, idx_body, 0)

    for nc in range(1, t_len // KV_CHUNK + 1):
        @pl.when(n_chunks == nc)
        def _(nc=nc):
            thr_ref[...] = _kth_largest_key(key_ref, topk, nc * KV_CHUNK)

    thr = thr_ref[...]

    def sel_body(c, carry):
        n_gt, n_eq = carry
        ks = pl.multiple_of(c * KV_CHUNK, KV_CHUNK)
        key = key_ref[:, pl.ds(ks, KV_CHUNK)]
        bias_ref[:, pl.ds(ks, KV_CHUNK)] = jnp.where(ks + lane <= t, jnp.where(key >= thr, 0.0, NEG), NEG)
        return n_gt + _count(key > thr), n_eq + _count(key == thr)

    zeros = jnp.zeros((q_blk, 1), F32)
    n_gt, n_eq = lax.fori_loop(0, n_chunks, sel_body, (zeros, zeros))
    overflow = jnp.where((n_gt + n_eq > float(topk)) & (thr > NEG_KEY), 1.0, 0.0)

    @pl.when(jnp.max(overflow) > 0.0)
    def _():
        pw = KV_CHUNK // 2
        upper = _prefix_upper(pw)
        quota = float(topk) - n_gt
        lane_p = lax.broadcasted_iota(jnp.int32, (1, pw), 1)

        def tie_body(c, used):
            ks = pl.multiple_of(c * pw, pw)
            kc = key_ref[:, pl.ds(ks, pw)]
            eq = kc == thr
            pre = _dot(jnp.where(eq, 1.0, 0.0).astype(BF16), upper) + used
            keep = jnp.where(eq, jnp.where(pre <= quota, 0.0, NEG), jnp.where(kc > thr, 0.0, NEG))
            bias_ref[:, pl.ds(ks, pw)] = jnp.where(ks + lane_p <= t, keep, NEG)
            return pre[:, pw - 1:pw]

        lax.fori_loop(0, n_chunks * (KV_CHUNK // pw), tie_body, zeros)

    q = q_ref[...].reshape(HEADS * q_blk, HD)

    def att_body(c, carry):
        ks = pl.multiple_of(c * KV_CHUNK, KV_CHUNK)
        return _attn_chunk(carry, q, k_ref[pl.ds(ks, KV_CHUNK), :], v_ref[pl.ds(ks, KV_CHUNK), :],
                           bias_ref[:, pl.ds(ks, KV_CHUNK)], q_blk)

    out = lax.fori_loop(0, n_chunks, att_body, _attn_init(HEADS * q_blk, HD))
    o_ref[...] = _heads_to_lanes(_attn_result(out), q_blk).astype(BF16)


def _dsa_prompt(q3, idx_w, kv7, bsz, t_len):
    n = bsz * t_len
    q_blk = DSA_Q_BLK
    nq = t_len // q_blk
    topk = min(DSA_TOPK, t_len // 4)
    assert t_len % q_blk == 0 and topk <= KV_CHUNK
    qspec = lambda g: pl.BlockSpec((None, HEADS, q_blk, HD), lambda b, i: (g, 0, b * nq + i, 0))
    kspec = lambda j: pl.BlockSpec((None, t_len, HD), lambda b, i: (j, b, 0))
    return pl.pallas_call(
        functools.partial(_dsa_prompt_body, t_len=t_len, topk=topk),
        grid=(bsz, nq),
        in_specs=[qspec(1), pl.BlockSpec((q_blk, HEADS), lambda b, i: (b * nq + i, 0)), qspec(0),
                  kspec(2), kspec(0), kspec(1)],
        out_specs=pl.BlockSpec((q_blk, HEADS * HD), lambda b, i: (b * nq + i, 0)),
        out_shape=jax.ShapeDtypeStruct((n, HEADS * HD), BF16),
        scratch_shapes=[pltpu.VMEM((q_blk, t_len), jnp.int32), pltpu.VMEM((q_blk, t_len), F32),
                        pltpu.VMEM((q_blk, 1), jnp.int32)],
        compiler_params=_cparams(2), name="dsa_prompt",
    )(q3, idx_w, q3, kv7, kv7, kv7)


def _mla_prompt_body(q_ref, k_ref, wuv_ref, o_ref):
    q_blk = Q_BLK
    qi = pl.program_id(1)
    n_chunks = (qi * q_blk + q_blk + KV_CHUNK - 1) // KV_CHUNK
    t = qi * q_blk + lax.broadcasted_iota(jnp.int32, (q_blk, 1), 0)
    lane = lax.broadcasted_iota(jnp.int32, (1, KV_CHUNK), 1)
    q = q_ref[...].reshape(HEADS * q_blk, MLA_ROW)

    def body(c, carry):
        ks = pl.multiple_of(c * KV_CHUNK, KV_CHUNK)
        rows = k_ref[pl.ds(ks, KV_CHUNK), :]
        return _attn_chunk(carry, q, rows, rows[:, :D_C], jnp.where(ks + lane <= t, 0.0, NEG), q_blk,
                           scale=(D_NOPE + D_ROPE) ** -0.5)

    lat = _attn_result(lax.fori_loop(0, n_chunks, body, _attn_init(HEADS * q_blk, D_C))).astype(BF16)
    o_ref[...] = jnp.concatenate(
        [_dot(lat[hh * q_blk:(hh + 1) * q_blk], wuv_ref[hh]) for hh in range(HEADS)], axis=1).astype(BF16)


def _mla_prompt(qcat, mla_bf, wuv, bsz, t_len):
    n = bsz * t_len
    nq = t_len // Q_BLK
    return pl.pallas_call(
        _mla_prompt_body, grid=(bsz, nq),
        in_specs=[pl.BlockSpec((HEADS, Q_BLK, MLA_ROW), lambda b, i: (0, b * nq + i, 0)),
                  pl.BlockSpec((t_len, MLA_ROW), lambda b, i: (b, 0)),
                  pl.BlockSpec((HEADS, D_C, D_V), lambda b, i: (0, 0, 0))],
        out_specs=pl.BlockSpec((Q_BLK, HEADS * D_V), lambda b, i: (b * nq + i, 0)),
        out_shape=jax.ShapeDtypeStruct((n, HEADS * D_V), BF16),
        compiler_params=_cparams(2), name="mla_prompt",
    )(qcat, mla_bf, wuv)


def _compress_blocks(read_rows, pos_ref, w1_ref, w2_ref, n_blk):
    acc = jnp.zeros((n_blk, 2 * CMP_HID), F32)
    for r in range(CMP_BLK):
        xr = (read_rows(r) + pos_ref[r:r + 1, :]).astype(BF16)
        acc = acc + _dot(xr, w1_ref[r])
    hid = jax.nn.gelu(acc).astype(BF16)
    return _dot(hid, w2_ref[...])


def _cmp_attend(q, ckv, vis):
    s = _nt_dot(q, ckv[:, :HD].astype(BF16))
    s = jnp.where(vis, s, NEG)
    e = jnp.exp(s - jnp.max(s, axis=1, keepdims=True))
    p = jnp.where(vis, e / jnp.sum(e, axis=1, keepdims=True), 0.0)
    return _dot(p.astype(BF16), ckv[:, HD:].astype(BF16)), p


def _nsa_prompt_body(q_ref, g_ref, rows_ref, sk_ref, sv_ref, wk_ref, wv_ref, pos_ref, w1_ref, w2_ref,
                     o_ref, ckv_ref, exp_ref, bias_ref, *, t_len):
    q_blk = Q_BLK
    n_cmp = t_len // CMP_BLK
    n_sel_blk = t_len // SEL_BLK
    ratio = SEL_BLK // CMP_BLK
    n_keep = min(N_SEL, n_sel_blk)
    qi = pl.program_id(1)
    t = qi * q_blk + lax.broadcasted_iota(jnp.int32, (q_blk, 1), 0)

    @pl.when(qi == 0)
    def _():
        ckv_ref[...] = _compress_blocks(lambda r: rows_ref[pl.ds(r, n_cmp, stride=CMP_BLK), :],
                                        pos_ref, w1_ref, w2_ref, n_cmp)
        j = lax.broadcasted_iota(jnp.int32, (n_cmp, t_len), 0)
        s = lax.broadcasted_iota(jnp.int32, (n_cmp, t_len), 1)
        exp_ref[...] = jnp.where(j == (s // SEL_BLK) * ratio, 1.0, 0.0).astype(BF16)

    q = q_ref[...].reshape(HEADS * q_blk, HD)
    g = g_ref[...]

    jc = lax.broadcasted_iota(jnp.int32, (1, n_cmp), 1)
    vis = (jc + 1) * CMP_BLK - 1 <= t
    vis8 = jnp.broadcast_to(vis[None], (HEADS, q_blk, n_cmp)).reshape(HEADS * q_blk, n_cmp)
    o_c, p = _cmp_attend(q, ckv_ref[...], vis8)
    imp = p[0:q_blk]
    for hh in range(1, HEADS):
        imp = imp + p[hh * q_blk:(hh + 1) * q_blk]
    assert ratio == 2
    width = -(-n_cmp // LANES) * LANES
    if width > n_cmp:
        imp = jnp.concatenate([imp, jnp.zeros((q_blk, width - n_cmp), F32)], axis=1)
    val = imp + pltpu.roll(imp, width - 1, 1)
    lw = lax.broadcasted_iota(jnp.int32, (1, width), 1)
    jb = lw // ratio
    cur = t // SEL_BLK
    val = jnp.where((jb == cur) | (jb == 0), FORCE, val)
    val = jnp.where(jb > cur, NEG, val)
    valid = (lw % ratio == 0) & (lw < n_cmp)
    val = jnp.where(valid, val, -jnp.inf)
    rank = jnp.zeros((q_blk, width), F32)
    for i in range(0, n_cmp, ratio):
        col = val[:, i:i + 1]
        rank = rank + jnp.where(lw > i, jnp.where(col >= val, 1.0, 0.0), jnp.where(col > val, 1.0, 0.0))
    selmask = jnp.where(valid & (rank < float(n_keep)), 1.0, 0.0)[:, :n_cmp].astype(BF16)
    n_chunks = (qi * q_blk + q_blk + KV_CHUNK - 1) // KV_CHUNK
    lane = lax.broadcasted_iota(jnp.int32, (1, KV_CHUNK), 1)

    def bias_body(c, carry):
        ks = pl.multiple_of(c * KV_CHUNK, KV_CHUNK)
        kept = _dot(selmask, exp_ref[:, pl.ds(ks, KV_CHUNK)]) > 0.5
        bias_ref[:, pl.ds(ks, KV_CHUNK)] = jnp.where(ks + lane <= t, jnp.where(kept, 0.0, NEG), NEG)
        return carry

    lax.fori_loop(0, n_chunks, bias_body, 0)

    def sel_body(c, carry):
        ks = pl.multiple_of(c * KV_CHUNK, KV_CHUNK)
        return _attn_chunk(carry, q, sk_ref[pl.ds(ks, KV_CHUNK), :], sv_ref[pl.ds(ks, KV_CHUNK), :],
                           bias_ref[:, pl.ds(ks, KV_CHUNK)], q_blk)

    o_s = _attn_result(lax.fori_loop(0, n_chunks, sel_body, _attn_init(HEADS * q_blk, HD)))

    wk_len = min(WINDOW + q_blk, t_len)
    start = pl.multiple_of(jnp.clip(qi * q_blk - WINDOW, 0, t_len - wk_len), LANES)
    kpos = start + lax.broadcasted_iota(jnp.int32, (1, wk_len), 1)
    o_w = _attn_result(_attn_chunk(
        _attn_init(HEADS * q_blk, HD), q, wk_ref[pl.ds(start, wk_len), :], wv_ref[pl.ds(start, wk_len), :],
        jnp.where(kpos <= t, jnp.where(kpos >= t - WINDOW, 0.0, NEG), NEG), q_blk))

    outs = []
    for hh in range(HEADS):
        r = slice(hh * q_blk, (hh + 1) * q_blk)
        outs.append(g[:, hh:hh + 1] * o_c[r] + g[:, HEADS + hh:HEADS + hh + 1] * o_s[r]
                    + g[:, 2 * HEADS + hh:2 * HEADS + hh + 1] * o_w[r])
    o_ref[...] = jnp.concatenate(outs, axis=1).astype(BF16)


def _nsa_prompt(q3, nsa_g, rows_nsa, kv7, pw, bsz, t_len):
    n = bsz * t_len
    nq = t_len // Q_BLK
    n_cmp = t_len // CMP_BLK
    kspec = lambda j: pl.BlockSpec((None, t_len, HD), lambda b, i: (j, b, 0))
    full = lambda shape: pl.BlockSpec(shape, lambda b, i: (0,) * len(shape))
    return pl.pallas_call(
        functools.partial(_nsa_prompt_body, t_len=t_len),
        grid=(bsz, nq),
        in_specs=[pl.BlockSpec((None, HEADS, Q_BLK, HD), lambda b, i: (2, 0, b * nq + i, 0)),
                  pl.BlockSpec((Q_BLK, 3 * HEADS), lambda b, i: (b * nq + i, 0)),
                  pl.BlockSpec((t_len, 2 * HD), lambda b, i: (b, 0)),
                  kspec(3), kspec(4), kspec(5), kspec(6),
                  full((CMP_BLK, 2 * HD)), full((CMP_BLK, 2 * HD, 2 * CMP_HID)), full((2 * CMP_HID, 2 * HD))],
        out_specs=pl.BlockSpec((Q_BLK, HEADS * HD), lambda b, i: (b * nq + i, 0)),
        out_shape=jax.ShapeDtypeStruct((n, HEADS * HD), BF16),
        scratch_shapes=[pltpu.VMEM((n_cmp, 2 * HD), F32), pltpu.VMEM((n_cmp, t_len), BF16),
                        pltpu.VMEM((Q_BLK, t_len), F32)],
        compiler_params=_cparams(2), name="nsa_prompt",
    )(q3, nsa_g, rows_nsa, kv7, kv7, kv7, kv7, pw['cmp_pos'], pw['w1bd'], pw['w2bd'])


def _merge_out_body(x_ref, g1_ref, wmg_ref, od_ref, on_ref, om_ref, wbr_ref, wo_ref, x1_ref):
    x = x_ref[...]
    d = x.shape[1]
    h = (x * lax.rsqrt(jnp.mean(x * x, axis=-1, keepdims=True) + EPS) * g1_ref[...]).astype(BF16)
    gate = jax.nn.sigmoid(_dot(h, wmg_ref[...]))
    mix = (gate[:, 0:d] * _dot(od_ref[...], wbr_ref[0]) + gate[:, d:2 * d] * _dot(on_ref[...], wbr_ref[1])
           + gate[:, 2 * d:3 * d] * _dot(om_ref[...], wbr_ref[2]))
    x1_ref[...] = x + _dot(mix.astype(BF16), wo_ref[...])


def _merge_out(x2d, o_dsa, o_nsa, o_mla, pw, tm):
    n, d = x2d.shape
    row = lambda w: pl.BlockSpec((tm, w), lambda i: (i, 0))
    full = lambda shape: pl.BlockSpec(shape, lambda i: (0,) * len(shape))
    return pl.pallas_call(
        _merge_out_body, grid=(n // tm,),
        in_specs=[row(d), full((1, d)), full((d, 3 * d)), row(HEADS * HD), row(HEADS * HD), row(HEADS * D_V),
                  full((3, HEADS * HD, d)), full((d, d))],
        out_specs=row(d), out_shape=jax.ShapeDtypeStruct((n, d), F32),
        compiler_params=_cparams(1), name="merge_out",
    )(x2d, pw['g1'], pw['wmg'], o_dsa, o_nsa, o_mla, pw['wbr'], pw['wo'])


def _ffn_body(x_ref, g2_ref, wg_ref, wu_ref, wd_ref, gf_ref, x2_ref, *maybe_y_ref):
    x = x_ref[...]
    h = (x * lax.rsqrt(jnp.mean(x * x, axis=-1, keepdims=True) + EPS) * g2_ref[...]).astype(BF16)
    act = (jax.nn.silu(_dot(h, wg_ref[...])) * _dot(h, wu_ref[...])).astype(BF16)
    x2 = x + _dot(act, wd_ref[...])
    x2_ref[...] = x2
    for y_ref in maybe_y_ref:
        y_ref[...] = x2 * lax.rsqrt(jnp.mean(x2 * x2, axis=-1, keepdims=True) + EPS) * gf_ref[...]


def _ffn(x1, pw, gf, tm, last):
    n, d = x1.shape
    dff = pw['wg'].shape[1]
    row = pl.BlockSpec((tm, d), lambda i: (i, 0))
    full = lambda shape: pl.BlockSpec(shape, lambda i: (0,) * len(shape))
    n_out = 2 if last else 1
    outs = pl.pallas_call(
        _ffn_body, grid=(n // tm,),
        in_specs=[row, full((1, d)), full((d, dff)), full((d, dff)), full((dff, d)), full((1, d))],
        out_specs=(row,) * n_out,
        out_shape=(jax.ShapeDtypeStruct((n, d), F32),) * n_out,
        compiler_params=_cparams(1), name="ffn",
    )(x1, pw['g2'], pw['wg'], pw['wu'], pw['wd'], gf)
    return (outs[0], outs[1]) if last else (outs[0], None)


def _page_copy(pool_ref, layer, page, lo, width, buf_ref, slot, j, sem_ref, feature_major):
    if feature_major:
        return pltpu.make_async_copy(pool_ref.at[layer, page, pl.ds(lo, width), :],
                                     buf_ref.at[slot, :, pl.ds(j * PAGE_SIZE, PAGE_SIZE)], sem_ref.at[slot])
    return pltpu.make_async_copy(pool_ref.at[layer, page, :, pl.ds(lo, width)],
                                 buf_ref.at[slot, pl.ds(j * PAGE_SIZE, PAGE_SIZE), :], sem_ref.at[slot])


def _stream_chunk(pt_ref, pool_ref, buf_ref, sem_ref, *, layer, lo, width, ppc, feature_major):
    b, c = pl.program_id(0), pl.program_id(1)
    nb, nc = pl.num_programs(0), pl.num_programs(1)
    step = b * nc + c
    slot = step % 2

    def start(bb, cc, sl):
        def body(j, carry):
            _page_copy(pool_ref, layer, pt_ref[bb, cc * ppc + j], lo, width, buf_ref, sl, j, sem_ref,
                       feature_major).start()
            return carry
        lax.fori_loop(0, ppc, body, 0)

    @pl.when(step == 0)
    def _():
        start(b, c, slot)

    @pl.when(step + 1 < nb * nc)
    def _():
        last = c + 1 == nc
        start(jnp.where(last, b + 1, b), jnp.where(last, 0, c + 1), 1 - slot)

    def wait_body(j, carry):
        _page_copy(pool_ref, layer, 0, lo, width, buf_ref, slot, j, sem_ref, feature_major).wait()
        return carry
    lax.fori_loop(0, ppc, wait_body, 0)
    return slot


def _pages_per_chunk(n_pages):
    ppc = min(64, n_pages)
    assert n_pages % ppc == 0
    return ppc


def _stream_call(body, pt, in_arrays, in_specs, pool, out_shape, out_specs, buf_width, scratch, name,
                 feature_major):
    bsz, n_pages = pt.shape
    ppc = _pages_per_chunk(n_pages)
    ch = ppc * PAGE_SIZE
    grid_spec = pltpu.PrefetchScalarGridSpec(
        num_scalar_prefetch=1, grid=(bsz, n_pages // ppc),
        in_specs=list(in_specs) + [pl.BlockSpec(memory_space=pl.ANY)],
        out_specs=out_specs,
        scratch_shapes=[pltpu.VMEM((2, buf_width, ch) if feature_major else (2, ch, buf_width), F32),
                        pltpu.SemaphoreType.DMA((2,))] + list(scratch))
    return pl.pallas_call(body, grid_spec=grid_spec, out_shape=out_shape, compiler_params=_cparams(2),
                          name=name)(pt, *in_arrays, pool)


def _dsa_scores_body(pt_ref, iq_ref, iw_ref, newk_ref, pool_ref, sc_ref, snew_ref, buf_ref, sem_ref, *, layer, ppc):
    slot = _stream_chunk(pt_ref, pool_ref, buf_ref, sem_ref, layer=layer, lo=2 * HD, width=IDX_DIM, ppc=ppc,
                         feature_major=True)
    iq = iq_ref[...]
    w = iw_ref[...] * (IDX_DIM ** -0.5)
    s = jnp.maximum(_dot(iq, buf_ref[slot].astype(BF16)), 0.0) * w
    sc_ref[...] = jnp.sum(s, axis=0, keepdims=True)

    @pl.when(pl.program_id(1) == pl.num_programs(1) - 1)
    def _():
        kn = newk_ref[...].astype(BF16).astype(F32)
        sn = jnp.maximum(jnp.sum(iq.astype(F32) * kn, axis=1, keepdims=True), 0.0) * w
        sn = jnp.sum(sn, axis=0, keepdims=True)
        lane = lax.broadcasted_iota(jnp.int32, (1, LANES), 1)
        snew_ref[...] = jnp.where(lane == 0, sn, NEG)


def _dsa_scores(pt, idx_q_s, idx_w_s, idx_k_new, cache, layer):
    bsz, n_pages = pt.shape
    ppc = _pages_per_chunk(n_pages)
    ch = ppc * PAGE_SIZE
    return _stream_call(
        functools.partial(_dsa_scores_body, layer=layer, ppc=ppc), pt,
        (idx_q_s, idx_w_s, idx_k_new),
        [pl.BlockSpec((None, HEADS, IDX_DIM), lambda b, c, pt: (b, 0, 0)),
         pl.BlockSpec((None, HEADS, 1), lambda b, c, pt: (b, 0, 0)),
         pl.BlockSpec((None, 1, IDX_DIM), lambda b, c, pt: (b, 0, 0))],
        cache,
        (jax.ShapeDtypeStruct((bsz, 1, n_pages * PAGE_SIZE), F32), jax.ShapeDtypeStruct((bsz, 1, LANES), F32)),
        (pl.BlockSpec((None, 1, ch), lambda b, c, pt: (b, 0, c)),
         pl.BlockSpec((None, 1, LANES), lambda b, c, pt: (b, 0, 0))),
        IDX_DIM, [], "dsa_scores", True)


def _dsa_thresh_body(sc_ref, snew_ref, thr_ref, quota_ref, key_ref, *, topk):
    past = sc_ref.shape[1]
    key_ref[:, 0:past] = _sort_key(sc_ref[...])
    key_ref[:, past:past + LANES] = _sort_key(snew_ref[...])
    thr = _kth_largest_key(key_ref, topk, past + LANES)
    key = key_ref[...]
    n_gt = _count(key > thr)
    n_eq = _count(key == thr)
    bits = thr ^ ((thr >> 31) & 0x7FFFFFFF)
    thr_ref[...] = jnp.broadcast_to(pltpu.bitcast(bits, F32), thr_ref.shape)
    quota = jnp.where(n_gt + n_eq > float(topk), float(topk) - n_gt, float(2 ** 24))
    quota_ref[...] = jnp.broadcast_to(quota, quota_ref.shape)


def _dsa_thresh(scores, snew, topk):
    bsz, past = scores.shape
    return pl.pallas_call(
        functools.partial(_dsa_thresh_body, topk=topk),
        out_shape=(jax.ShapeDtypeStruct((bsz, LANES), F32), jax.ShapeDtypeStruct((bsz, LANES), F32)),
        scratch_shapes=[pltpu.VMEM((bsz, past + LANES), jnp.int32)],
        compiler_params=pltpu.CompilerParams(vmem_limit_bytes=V7X_VMEM_LIMIT), name="dsa_thresh",
    )(scores, snew)


def _dsa_attend_body(pt_ref, q_ref, sc_ref, snew_ref, thr_ref, quota_ref, new_ref, pool_ref, o_ref,
                     buf_ref, sem_ref, m_ref, l_ref, acc_ref, used_ref, bias_ref, *, layer, ppc):
    slot = _stream_chunk(pt_ref, pool_ref, buf_ref, sem_ref, layer=layer, lo=0, width=2 * HD, ppc=ppc,
                         feature_major=True)
    c = pl.program_id(1)
    ch = ppc * PAGE_SIZE

    @pl.when(c == 0)
    def _():
        m_ref[...] = jnp.full(m_ref.shape, 2 * NEG, F32)
        l_ref[...] = jnp.zeros(l_ref.shape, F32)
        acc_ref[...] = jnp.zeros(acc_ref.shape, F32)
        used_ref[...] = jnp.zeros(used_ref.shape, F32)

    thr = thr_ref[:, 0:1]
    quota = quota_ref[:, 0:1]
    sc = sc_ref[...] + 0.0
    bias_ref[...] = jnp.where(sc >= thr, 0.0, NEG)

    @pl.when(jnp.max(quota_ref[...]) < float(2 ** 23))
    def _():
        pw = 512
        upper = _prefix_upper(pw)
        used = used_ref[...]
        for i in range(ch // pw):
            sci = sc[:, i * pw:(i + 1) * pw]
            eq = sci == thr
            e8 = jnp.broadcast_to(jnp.where(eq, 1.0, 0.0), (8, pw)).astype(BF16)
            pre = _dot(e8, upper)[0:1] + used
            bias_ref[:, i * pw:(i + 1) * pw] = jnp.where((sci > thr) | (eq & (pre <= quota)), 0.0, NEG)
            used = pre[:, pw - 1:pw]
        used_ref[...] = used

    q = q_ref[...]
    kv = buf_ref[slot].astype(BF16)
    s = _dot(q, kv[0:HD]) + bias_ref[...]
    m, l, alpha, p = _online_step(s, m_ref[...], l_ref[...])
    acc = alpha * acc_ref[...] + _nt_dot(p, kv[HD:2 * HD])
    m_ref[...], l_ref[...], acc_ref[...] = m, l, acc

    @pl.when(c == pl.num_programs(1) - 1)
    def _():
        new = new_ref[...].astype(BF16).astype(F32)
        sn = snew_ref[:, 0:1] + 0.0
        sel_n = (sn > thr) | ((sn == thr) & (used_ref[...] + 1.0 <= quota))
        s_n = jnp.where(sel_n, jnp.sum(q.astype(F32) * new[:, :HD], axis=1, keepdims=True), NEG)
        m_new = jnp.maximum(m, s_n)
        alpha = jnp.exp(m - m_new)
        p_n = jnp.exp(s_n - m_new).astype(BF16).astype(F32)
        o = (alpha * acc + p_n * new[:, HD:2 * HD]) / (alpha * l + p_n)
        o_ref[...] = o.astype(BF16)


def _dsa_attend(pt, dsa_q_s, scores, snew, thr, quota, rows_new, cache, layer):
    bsz, n_pages = pt.shape
    ppc = _pages_per_chunk(n_pages)
    ch = ppc * PAGE_SIZE
    per_b = lambda w: pl.BlockSpec((None, 1, w), lambda b, c, pt: (b, 0, 0))
    return _stream_call(
        functools.partial(_dsa_attend_body, layer=layer, ppc=ppc), pt,
        (dsa_q_s, scores, snew, thr, quota, rows_new),
        [pl.BlockSpec((None, HEADS, HD), lambda b, c, pt: (b, 0, 0)),
         pl.BlockSpec((None, 1, ch), lambda b, c, pt: (b, 0, c)),
         per_b(LANES), per_b(LANES), per_b(LANES), per_b(DSA_ROW)],
        cache,
        jax.ShapeDtypeStruct((bsz, HEADS, HD), BF16),
        pl.BlockSpec((None, HEADS, HD), lambda b, c, pt: (b, 0, 0)),
        2 * HD,
        [pltpu.VMEM((HEADS, 1), F32), pltpu.VMEM((HEADS, 1), F32), pltpu.VMEM((HEADS, HD), F32),
         pltpu.VMEM((1, 1), F32), pltpu.VMEM((1, ch), F32)],
        "dsa_attend", True)


def _mla_decode_body(pt_ref, q_ref, new_ref, wuv_ref, pool_ref, o_ref, buf_ref, sem_ref, m_ref, l_ref, acc_ref,
                     *, layer, ppc):
    slot = _stream_chunk(pt_ref, pool_ref, buf_ref, sem_ref, layer=layer, lo=0, width=MLA_ROW, ppc=ppc,
                         feature_major=True)
    c = pl.program_id(1)
    scale = (D_NOPE + D_ROPE) ** -0.5

    @pl.when(c == 0)
    def _():
        m_ref[...] = jnp.full(m_ref.shape, 2 * NEG, F32)
        l_ref[...] = jnp.zeros(l_ref.shape, F32)
        acc_ref[...] = jnp.zeros(acc_ref.shape, F32)

    q = q_ref[...]
    rows = buf_ref[slot].astype(BF16)
    s = _dot(q, rows) * scale
    m, l, alpha, p = _online_step(s, m_ref[...], l_ref[...])
    acc = alpha * acc_ref[...] + _nt_dot(p, rows[0:D_C])
    m_ref[...], l_ref[...], acc_ref[...] = m, l, acc

    @pl.when(c == pl.num_programs(1) - 1)
    def _():
        new = new_ref[...].astype(BF16).astype(F32)
        s_n = jnp.sum(q.astype(F32) * new, axis=1, keepdims=True) * scale
        m_new = jnp.maximum(m, s_n)
        alpha = jnp.exp(m - m_new)
        p_n = jnp.exp(s_n - m_new).astype(BF16).astype(F32)
        lat = ((alpha * acc + p_n * new[:, :D_C]) / (alpha * l + p_n)).astype(BF16)
        full = _dot(lat, wuv_ref[...])
        hrow = lax.broadcasted_iota(jnp.int32, full.shape, 0)
        hcol = lax.broadcasted_iota(jnp.int32, full.shape, 1) // D_V
        o_ref[...] = jnp.sum(jnp.where(hrow == hcol, full, 0.0), axis=0, keepdims=True).astype(BF16)


def _mla_decode(pt, qcat_s, rows_new, wuv_flat, cache, layer):
    bsz, n_pages = pt.shape
    ppc = _pages_per_chunk(n_pages)
    return _stream_call(
        functools.partial(_mla_decode_body, layer=layer, ppc=ppc), pt,
        (qcat_s, rows_new, wuv_flat),
        [pl.BlockSpec((None, HEADS, MLA_ROW), lambda b, c, pt: (b, 0, 0)),
         pl.BlockSpec((None, 1, MLA_ROW), lambda b, c, pt: (b, 0, 0)),
         pl.BlockSpec((D_C, HEADS * D_V), lambda b, c, pt: (0, 0))],
        cache,
        jax.ShapeDtypeStruct((bsz, 1, HEADS * D_V), BF16),
        pl.BlockSpec((None, 1, HEADS * D_V), lambda b, c, pt: (b, 0, 0)),
        MLA_ROW,
        [pltpu.VMEM((HEADS, 1), F32), pltpu.VMEM((HEADS, 1), F32), pltpu.VMEM((HEADS, D_C), F32)],
        "mla_decode", True)


def _nsa_cmp_body(pt_ref, q_ref, pos_ref, w1_ref, w2_ref, pool_ref, oc_ref, imp_ref, buf_ref, sem_ref, ckv_ref,
                  *, layer, ppc):
    slot = _stream_chunk(pt_ref, pool_ref, buf_ref, sem_ref, layer=layer, lo=0, width=2 * HD, ppc=ppc,
                         feature_major=False)
    c = pl.program_id(1)
    n_blk = ppc * PAGE_SIZE // CMP_BLK
    out = _compress_blocks(lambda r: buf_ref[slot, pl.ds(r, n_blk, stride=CMP_BLK), :], pos_ref, w1_ref, w2_ref,
                           n_blk)
    ckv_ref[pl.ds(pl.multiple_of(c * n_blk, n_blk), n_blk), :] = out

    @pl.when(c == pl.num_programs(1) - 1)
    def _():
        n_cmp = ckv_ref.shape[0]
        vis = jnp.full((HEADS, n_cmp), True)
        o_c, p = _cmp_attend(q_ref[...], ckv_ref[...], vis)
        oc_ref[...] = o_c
        imp = jnp.sum(p, axis=0, keepdims=True)
        imp_ref[...] = imp + pltpu.roll(imp, n_cmp - 1, 1)


def _nsa_cmp(pt, nsa_q_s, pw, cache, layer):
    bsz, n_pages = pt.shape
    ppc = _pages_per_chunk(n_pages)
    n_cmp = n_pages * PAGE_SIZE // CMP_BLK
    full = lambda shape: pl.BlockSpec(shape, lambda b, c, pt: (0,) * len(shape))
    return _stream_call(
        functools.partial(_nsa_cmp_body, layer=layer, ppc=ppc), pt,
        (nsa_q_s, pw['cmp_pos'], pw['w1bd'], pw['w2bd']),
        [pl.BlockSpec((None, HEADS, HD), lambda b, c, pt: (b, 0, 0)),
         full((CMP_BLK, 2 * HD)), full((CMP_BLK, 2 * HD, 2 * CMP_HID)), full((2 * CMP_HID, 2 * HD))],
        cache,
        (jax.ShapeDtypeStruct((bsz, HEADS, HD), F32), jax.ShapeDtypeStruct((bsz, 1, n_cmp), F32)),
        (pl.BlockSpec((None, HEADS, HD), lambda b, c, pt: (b, 0, 0)),
         pl.BlockSpec((None, 1, n_cmp), lambda b, c, pt: (b, 0, 0))),
        2 * HD, [pltpu.VMEM((n_cmp, 2 * HD), F32)], "nsa_cmp", False)


def _nsa_pick_body(imp_ref, ids_ref, *, n_pick):
    imp = imp_ref[...]
    width = imp.shape[1]
    ratio = SEL_BLK // CMP_BLK
    lane = lax.broadcasted_iota(jnp.int32, imp.shape, 1)
    val = jnp.where((lane % ratio == 0) & (lane >= ratio), imp, -jnp.inf)
    slot = lax.broadcasted_iota(jnp.int32, ids_ref.shape, 1)
    ids = jnp.zeros(ids_ref.shape, jnp.int32)
    for i in range(n_pick):
        m = jnp.max(val, axis=1, keepdims=True)
        idx = jnp.min(jnp.where(val == m, lane, width), axis=1, keepdims=True)
        ids = jnp.where(slot == i + 1, idx // ratio, ids)
        val = jnp.where(lane == idx, -jnp.inf, val)
    ids_ref[...] = ids


def _nsa_pick(imp, n_pick):
    bsz = imp.shape[0]
    return pl.pallas_call(
        functools.partial(_nsa_pick_body, n_pick=n_pick),
        out_shape=jax.ShapeDtypeStruct((bsz, LANES), jnp.int32), name="nsa_pick",
    )(imp)


def _nsa_sel_body(pt_ref, ids_ref, q_ref, g_ref, oc_ref, new_ref, newwin_ref, win_ref, pool_ref,
                  o_ref, wout_ref, buf_ref, sem_ref, *, layer, n_blk):
    b = pl.program_id(0)
    nb = pl.num_programs(0)
    slot = b % 2
    per_page = PAGE_SIZE // SEL_BLK

    def copy(bb, i, sl):
        blk = ids_ref[bb, i]
        page = pt_ref[bb, blk // per_page]
        return pltpu.make_async_copy(
            pool_ref.at[layer, page, pl.ds((blk % per_page) * SEL_BLK, SEL_BLK), pl.ds(2 * HD, 2 * HD)],
            buf_ref.at[sl, pl.ds(i * SEL_BLK, SEL_BLK), :], sem_ref.at[sl])

    def start(bb, sl):
        for i in range(n_blk):
            copy(bb, i, sl).start()

    @pl.when(b == 0)
    def _():
        start(b, slot)

    @pl.when(b + 1 < nb)
    def _():
        start(b + 1, 1 - slot)

    for i in range(n_blk):
        copy(b, i, slot).wait()

    q = q_ref[...]
    qf = q.astype(F32)

    def attend(rows, new):
        rows = rows.astype(BF16)
        new = new.astype(BF16).astype(F32)
        s = _nt_dot(q, rows[:, :HD])
        s_n = jnp.sum(qf * new[:, :HD], axis=1, keepdims=True)
        m = jnp.maximum(jnp.max(s, axis=1, keepdims=True), s_n)
        e = jnp.exp(s - m)
        e_n = jnp.exp(s_n - m)
        num = _dot(e.astype(BF16), rows[:, HD:]) + e_n.astype(BF16).astype(F32) * new[:, HD:]
        return num / (jnp.sum(e, axis=1, keepdims=True) + e_n)

    new = new_ref[...]
    o_s = attend(buf_ref[slot], new[:, 2 * HD:])
    win = win_ref[...]
    newwin = newwin_ref[...]
    o_w = attend(win, newwin)
    g = g_ref[...]
    o_ref[...] = (g[:, 0:1] * oc_ref[...] + g[:, 1:2] * o_s + g[:, 2:3] * o_w).astype(BF16)
    w = win.shape[0]
    wout_ref[0:w - 1, :] = win[1:w, :]
    wout_ref[w - 1:w, :] = newwin


def _nsa_sel(pt, ids, nsa_q_s, gates_s, o_c, rows_new, win_new, win_buf, cache, layer, n_blk):
    bsz = pt.shape[0]
    w = win_buf.shape[1]
    per_b = lambda r, c: pl.BlockSpec((None, r, c), lambda b, pt, ids: (b, 0, 0))
    grid_spec = pltpu.PrefetchScalarGridSpec(
        num_scalar_prefetch=2, grid=(bsz,),
        in_specs=[per_b(HEADS, HD), per_b(HEADS, 3), per_b(HEADS, HD), per_b(1, NSA_ROW), per_b(1, WIN_ROW),
                  per_b(w, WIN_ROW), pl.BlockSpec(memory_space=pl.ANY)],
        out_specs=(per_b(HEADS, HD), per_b(w, WIN_ROW)),
        scratch_shapes=[pltpu.VMEM((2, n_blk * SEL_BLK, 2 * HD), F32), pltpu.SemaphoreType.DMA((2,))])
    return pl.pallas_call(
        functools.partial(_nsa_sel_body, layer=layer, n_blk=n_blk), grid_spec=grid_spec,
        out_shape=(jax.ShapeDtypeStruct((bsz, HEADS, HD), BF16), jax.ShapeDtypeStruct((bsz, w, WIN_ROW), F32)),
        compiler_params=_cparams(1), name="nsa_sel",
    )(pt, ids, nsa_q_s, gates_s, o_c, rows_new, win_new, win_buf, cache)


def _layer_prompt(x2d, pw, gf, tabs64, tabs32, bsz, t_len, last):
    tm = 256
    (q3, qcat, idx_w, nsa_g, rows_dsa, rows_nsa, rows_mla, win, kv7, mla_bf) = _in_proj(
        x2d, pw, tabs64, tabs32, tm, t_len // tm)
    o_dsa = _dsa_prompt(q3, idx_w, kv7, bsz, t_len)
    o_nsa = _nsa_prompt(q3, nsa_g, rows_nsa, kv7, pw, bsz, t_len)
    o_mla = _mla_prompt(qcat, mla_bf, pw['wuv'], bsz, t_len)
    x1 = _merge_out(x2d, o_dsa, o_nsa, o_mla, pw, tm)
    x2, y = _ffn(x1, pw, gf, tm, last)
    return x2, y, rows_dsa, rows_nsa, rows_mla, win


def _layer_sample(x2d, pw, gf, tabs64, tabs32, layer, cache_dsa, cache_nsa, cache_mla, win_buf, pt, last):
    bsz = x2d.shape[0]
    n_pages = pt.shape[1]
    past = n_pages * PAGE_SIZE
    (q3, qcat, idx_w, nsa_g, rows_dsa, rows_nsa, rows_mla, win, _, _) = _in_proj(x2d, pw, tabs64, tabs32, bsz, 1)
    heads_first = lambda a: jnp.transpose(a, (1, 0, 2))
    dsa_q, idx_q, nsa_q = heads_first(q3[0]), heads_first(q3[1]), heads_first(q3[2])
    qcat_s = heads_first(qcat)
    scores, snew = _dsa_scores(pt, idx_q, idx_w[:, :, None], rows_dsa[:, None, 2 * HD:], cache_dsa, layer)
    topk = min(DSA_TOPK, (past + 1) // 4)
    thr, quota = _dsa_thresh(scores[:, 0, :], snew[:, 0, :], topk)
    o_dsa = _dsa_attend(pt, dsa_q, scores, snew, thr[:, None, :], quota[:, None, :], rows_dsa[:, None, :],
                        cache_dsa, layer)
    n_past_blk = past // SEL_BLK
    n_pick = min(N_SEL, n_past_blk + 1) - 2
    assert past % SEL_BLK == 0 and 0 <= n_pick <= n_past_blk - 1
    o_c, imp = _nsa_cmp(pt, nsa_q, pw, cache_nsa, layer)
    ids = _nsa_pick(imp[:, 0, :], n_pick)[:, :n_pick + 1]
    gates = jnp.transpose(nsa_g.reshape(bsz, 3, HEADS), (0, 2, 1))
    o_nsa, win_out = _nsa_sel(pt, ids, nsa_q, gates, o_c, rows_nsa[:, None, :], win[:, None, :], win_buf,
                              cache_nsa, layer, n_pick + 1)
    wuv_flat = jnp.transpose(pw['wuv'], (1, 0, 2)).reshape(D_C, HEADS * D_V)
    o_mla = _mla_decode(pt, qcat_s, rows_mla[:, None, :], wuv_flat, cache_mla, layer)
    x1 = _merge_out(x2d, o_dsa.reshape(bsz, HEADS * HD), o_nsa.reshape(bsz, HEADS * HD),
                    o_mla.reshape(bsz, HEADS * D_V), pw, bsz)
    x2, y = _ffn(x1, pw, gf, bsz, last)
    return x2, y, rows_dsa, rows_nsa, rows_mla, win_out


def kernel(x_prompt, x_sample, cache_dsa, cache_nsa, cache_mla, state_nsa_win, page_table, norm1_g, w_in,
           mla_q_norm_g, w_uq, mla_kv_norm_g, w_uk, w_uv, cmp_pos, cmp_w1, cmp_w2, w_br_dsa, w_br_nsa, w_br_mla,
           w_o, norm2_g, w_ff_gate, w_ff_up, w_ff_down, final_norm_g):
    bsz, t_len, d = x_prompt.shape
    dec_b, dec_t, _ = x_sample.shape
    depth = w_in.shape[0]
    past = page_table.shape[1] * PAGE_SIZE
    assert dec_t == 1 and t_len % KV_CHUNK == 0 and state_nsa_win.shape[2] == min(WINDOW, past)
    cache_dsa = jnp.swapaxes(cache_dsa, 2, 3)
    cache_mla = jnp.swapaxes(cache_mla, 2, 3)
    pos_p = jnp.arange(t_len, dtype=jnp.int32)
    pos_s = jnp.full((dec_b,), past, dtype=jnp.int32)
    tabs_p = (_rope_tables(pos_p, HD), _rope_tables(pos_p, D_ROPE))
    tabs_s = (_rope_tables(pos_s, HD), _rope_tables(pos_s, D_ROPE))
    gf = final_norm_g[None, :]
    xp = x_prompt.reshape(bsz * t_len, d)
    xs = x_sample.reshape(dec_b, d)
    outs = [[] for _ in range(8)]
    yp = ys = None
    for l in range(depth):
        lw = dict(norm1_g=norm1_g[l], w_in=w_in[l], mla_q_norm_g=mla_q_norm_g[l], w_uq=w_uq[l],
                  mla_kv_norm_g=mla_kv_norm_g[l], w_uk=w_uk[l], w_uv=w_uv[l], cmp_pos=cmp_pos[l],
                  cmp_w1=cmp_w1[l], cmp_w2=cmp_w2[l], w_br_dsa=w_br_dsa[l], w_br_nsa=w_br_nsa[l],
                  w_br_mla=w_br_mla[l], w_o=w_o[l], norm2_g=norm2_g[l], w_ff_gate=w_ff_gate[l],
                  w_ff_up=w_ff_up[l], w_ff_down=w_ff_down[l])
        pw = _prep_layer(lw)
        xp, yp, r_dsa, r_nsa, r_mla, r_win = _layer_prompt(xp, pw, gf, *tabs_p, bsz, t_len, l == depth - 1)
        wlen = min(WINDOW, t_len)
        outs[0].append(r_dsa.reshape(bsz, t_len, DSA_ROW))
        outs[2].append(r_nsa.reshape(bsz, t_len, NSA_ROW))
        outs[4].append(r_mla.reshape(bsz, t_len, MLA_ROW))
        outs[6].append(r_win.reshape(bsz, t_len, WIN_ROW)[:, t_len - wlen:])
        xs, ys, r_dsa, r_nsa, r_mla, r_win = _layer_sample(xs, pw, gf, *tabs_s, l, cache_dsa, cache_nsa, cache_mla,
                                                           state_nsa_win[l], page_table, l == depth - 1)
        outs[1].append(r_dsa.reshape(dec_b, 1, DSA_ROW))
        outs[3].append(r_nsa.reshape(dec_b, 1, NSA_ROW))
        outs[5].append(r_mla.reshape(dec_b, 1, MLA_ROW))
        outs[7].append(r_win)
    return (yp.reshape(bsz, t_len, d), ys.reshape(dec_b, 1, d), *(jnp.stack(o) for o in outs))
```

```python
import functools
import struct

import jax
import jax.numpy as jnp
from jax import lax
from jax.experimental import pallas as pl
from jax.experimental.pallas import tpu as pltpu

F32 = jnp.float32
BF16 = jnp.bfloat16

HD = 64
HEADS = 8
IDX_DIM = 64
DSA_TOPK = 256
CMP_BLK = 32
CMP_HID = 2 * HD
SEL_BLK = 64
N_SEL = 16
WINDOW = 512
D_CQ = 256
D_C = 128
D_NOPE = 64
D_ROPE = 32
D_V = 64
ROPE_THETA = 10000.0
EPS = 1e-6
NEG = -1e30
FORCE = 1e9
PAGE_SIZE = 128

DSA_ROW = 2 * HD + IDX_DIM
NSA_ROW = 4 * HD
MLA_ROW = D_C + D_ROPE
WIN_ROW = 2 * HD

IN_SPLITS = (
    ('dsa_q', HEADS * HD), ('dsa_k', HD), ('dsa_v', HD),
    ('idx_q', HEADS * IDX_DIM), ('idx_k', IDX_DIM), ('idx_w', HEADS),
    ('nsa_q', HEADS * HD), ('cmp_k', HD), ('cmp_v', HD), ('sel_k', HD), ('sel_v', HD),
    ('win_k', HD), ('win_v', HD), ('nsa_g', 3 * HEADS),
    ('mla_cq', D_CQ), ('mla_ckv', D_C), ('mla_kr', D_ROPE),
)

LANES = 128
V7X_VMEM_LIMIT = 56 * 1024 * 1024

Q_BLK = 128
DSA_Q_BLK = 256
KV_CHUNK = 512
BLK_PITCH = 40
ROPE_W = 15 * LANES
PLAIN_W = 6 * LANES


def _f32_key(v):
    bits = struct.unpack('<i', struct.pack('<f', v))[0]
    return bits ^ ((bits >> 31) & 0x7FFFFFFF)


NEG_KEY = _f32_key(NEG)
INT_MIN = -2 ** 31


def _sort_key(x):
    bits = pltpu.bitcast(x + 0.0, jnp.int32)
    return bits ^ ((bits >> 31) & 0x7FFFFFFF)


def _nt_dot(a, b):
    return lax.dot_general(a, b, (((1,), (1,)), ((), ())), preferred_element_type=F32)


def _dot(a, b):
    return jnp.dot(a, b, preferred_element_type=F32)


def _rope128(x, c, sa, sb, half):
    return x * c + pltpu.roll(x, LANES - half, 1) * sa + pltpu.roll(x, half, 1) * sb


def _cparams(n_axes):
    return pltpu.CompilerParams(dimension_semantics=("arbitrary",) * n_axes, vmem_limit_bytes=V7X_VMEM_LIMIT)


def _in_proj_body(x_ref, g1_ref, wr_ref, wp_ref, c64_ref, sa64_ref, sb64_ref, c32_ref, sa32_ref, sb32_ref,
                  gq_ref, wuq_ref, wuk_ref, gkv_ref,
                  q3_ref, qcat_ref, idxw_ref, nsag_ref, rdsa_ref, rnsa_ref, rmla_ref, win_ref, kv7_ref, mlabf_ref):
    x = x_ref[...]
    h = (x * lax.rsqrt(jnp.mean(x * x, axis=-1, keepdims=True) + EPS) * g1_ref[...]).astype(BF16)
    a = _dot(h, wr_ref[...])
    b = _dot(h, wp_ref[...])
    c64, sa64, sb64 = c64_ref[...], sa64_ref[...], sb64_ref[...]
    c32, sa32, sb32 = c32_ref[...], sa32_ref[...], sb32_ref[...]
    rg = [_rope128(a[:, g * LANES:(g + 1) * LANES], c64, sa64, sb64, HD // 2) for g in range(ROPE_W // LANES)]
    for grp, qscale in enumerate((HD ** -0.5, 1.0, HD ** -0.5)):
        for hh in range(HEADS):
            g, o = grp * 4 + hh // 2, (hh % 2) * HD
            q3_ref[grp, hh] = (rg[g][:, o:o + HD] * qscale).astype(BF16)
    dsa_k, idx_k = rg[12][:, :HD], rg[12][:, HD:]
    cmp_k, sel_k = rg[13][:, :HD], rg[13][:, HD:]
    win_k = rg[14][:, :HD]
    dsa_v, cmp_v, sel_v, win_v = (b[:, i * HD:(i + 1) * HD] for i in range(4))
    rdsa_ref[:, 0:HD] = dsa_k
    rdsa_ref[:, HD:2 * HD] = dsa_v
    rdsa_ref[:, 2 * HD:3 * HD] = idx_k
    rnsa_ref[:, 0:HD] = cmp_k
    rnsa_ref[:, HD:2 * HD] = cmp_v
    rnsa_ref[:, 2 * HD:3 * HD] = sel_k
    rnsa_ref[:, 3 * HD:4 * HD] = sel_v
    win_ref[:, 0:HD] = win_k
    win_ref[:, HD:2 * HD] = win_v
    for i, t in enumerate((dsa_k, dsa_v, idx_k, sel_k, sel_v, win_k, win_v)):
        kv7_ref[i] = t.astype(BF16)
    ckv = b[:, 2 * LANES:3 * LANES]
    ckv = ckv * lax.rsqrt(jnp.mean(ckv * ckv, axis=-1, keepdims=True) + EPS) * gkv_ref[...]
    tail = b[:, 5 * LANES:6 * LANES]
    kr = _rope128(tail, c32, sa32, sb32, D_ROPE // 2)[:, :D_ROPE]
    rmla_ref[:, 0:D_C] = ckv
    rmla_ref[:, D_C:MLA_ROW] = kr
    mlabf_ref[:, 0:D_C] = ckv.astype(BF16)
    mlabf_ref[:, D_C:MLA_ROW] = kr.astype(BF16)
    idxw_ref[...] = tail[:, D_ROPE:D_ROPE + HEADS] * (HEADS ** -0.5)
    nsag_ref[...] = jax.nn.sigmoid(tail[:, D_ROPE + HEADS:D_ROPE + 4 * HEADS])
    cq = b[:, 3 * LANES:5 * LANES]
    cq = (cq * lax.rsqrt(jnp.mean(cq * cq, axis=-1, keepdims=True) + EPS) * gq_ref[...]).astype(BF16)
    qm = _dot(cq, wuq_ref[...])
    qlat = _dot(qm[:, :HEADS * D_NOPE].astype(BF16), wuk_ref[...])
    qr = [_rope128(qm[:, HEADS * D_NOPE + g * LANES:HEADS * D_NOPE + (g + 1) * LANES], c32, sa32, sb32, D_ROPE // 2)
          for g in range(HEADS * D_ROPE // LANES)]
    per = LANES // D_ROPE
    for hh in range(HEADS):
        qcat_ref[hh, :, 0:D_C] = qlat[:, hh * D_C:(hh + 1) * D_C].astype(BF16)
        o = (hh % per) * D_ROPE
        qcat_ref[hh, :, D_C:MLA_ROW] = qr[hh // per][:, o:o + D_ROPE].astype(BF16)


def _rope_tables(pos, d):
    half = d // 2
    inv = ROPE_THETA ** (-jnp.arange(half, dtype=F32) * 2.0 / d)
    ang = pos.astype(F32)[:, None] * inv[None, :]
    cos, sin = jnp.cos(ang), jnp.sin(ang)
    z = jnp.zeros_like(sin)
    rep = LANES // d
    return (jnp.tile(jnp.concatenate([cos, cos], 1), (1, rep)),
            jnp.tile(jnp.concatenate([-sin, z], 1), (1, rep)),
            jnp.tile(jnp.concatenate([z, sin], 1), (1, rep)))


def _split_w_in(w_in_l):
    cols, off = {}, 0
    for name, w in IN_SPLITS:
        cols[name] = w_in_l[:, off:off + w]
        off += w
    cols['merge_g'] = w_in_l[:, off:]
    return cols


def _prep_layer(lw):
    c = _split_w_in(lw['w_in'])
    d = lw['w_in'].shape[0]
    z64 = jnp.zeros((d, HD), F32)
    wr = jnp.concatenate([c['dsa_q'], c['idx_q'], c['nsa_q'], c['dsa_k'], c['idx_k'], c['cmp_k'], c['sel_k'],
                          c['win_k'], z64], axis=1)
    wp = jnp.concatenate([c['dsa_v'], c['cmp_v'], c['sel_v'], c['win_v'], c['mla_ckv'], c['mla_cq'],
                          c['mla_kr'], c['idx_w'], c['nsa_g'], z64], axis=1)
    assert wr.shape[1] == ROPE_W and wp.shape[1] == PLAIN_W
    wuq = lw['w_uq'].reshape(D_CQ, HEADS, D_NOPE + D_ROPE)
    wuq = jnp.concatenate([wuq[:, :, :D_NOPE].reshape(D_CQ, HEADS * D_NOPE),
                           wuq[:, :, D_NOPE:].reshape(D_CQ, HEADS * D_ROPE)], axis=1)
    wk = jnp.transpose(lw['w_uk'], (1, 2, 0))
    wuk_bd = (jnp.eye(HEADS, dtype=F32)[:, None, :, None] * wk[:, :, None, :]).reshape(HEADS * D_NOPE, HEADS * D_C)
    wuv = jnp.transpose(lw['w_uv'], (1, 0, 2))
    w1 = lw['cmp_w1'].reshape(2, CMP_BLK, HD, CMP_HID)
    zz = jnp.zeros((CMP_BLK, HD, CMP_HID), F32)
    w1bd = jnp.concatenate([jnp.concatenate([w1[0], zz], axis=2), jnp.concatenate([zz, w1[1]], axis=2)], axis=1)
    z2 = jnp.zeros((CMP_HID, HD), F32)
    w2bd = jnp.concatenate([jnp.concatenate([lw['cmp_w2'][0], z2], axis=1),
                            jnp.concatenate([z2, lw['cmp_w2'][1]], axis=1)], axis=0)
    return dict(
        g1=lw['norm1_g'][None, :], wr=wr.astype(BF16), wp=wp.astype(BF16),
        gq=lw['mla_q_norm_g'][None, :], wuq=wuq.astype(BF16), wuk=wuk_bd.astype(BF16),
        gkv=lw['mla_kv_norm_g'][None, :], wuv=wuv.astype(BF16),
        cmp_pos=lw['cmp_pos'].reshape(CMP_BLK, 2 * HD), w1bd=w1bd.astype(BF16), w2bd=w2bd.astype(BF16),
        wmg=c['merge_g'].astype(BF16),
        wbr=jnp.stack([lw['w_br_dsa'], lw['w_br_nsa'], lw['w_br_mla']]).astype(BF16),
        wo=lw['w_o'].astype(BF16), g2=lw['norm2_g'][None, :],
        wg=lw['w_ff_gate'].astype(BF16), wu=lw['w_ff_up'].astype(BF16), wd=lw['w_ff_down'].astype(BF16),
    )


def _in_proj(x2d, pw, tabs64, tabs32, tm, n_pos_blocks):
    n, d = x2d.shape
    full = lambda shape: pl.BlockSpec(shape, lambda i: (0,) * len(shape))
    tab = pl.BlockSpec((tm, LANES), lambda i: (i % n_pos_blocks, 0))
    row = lambda w: pl.BlockSpec((tm, w), lambda i: (i, 0))
    out_shape = (
        jax.ShapeDtypeStruct((3, HEADS, n, HD), BF16),
        jax.ShapeDtypeStruct((HEADS, n, MLA_ROW), BF16),
        jax.ShapeDtypeStruct((n, HEADS), F32),
        jax.ShapeDtypeStruct((n, 3 * HEADS), F32),
        jax.ShapeDtypeStruct((n, DSA_ROW), F32),
        jax.ShapeDtypeStruct((n, NSA_ROW), F32),
        jax.ShapeDtypeStruct((n, MLA_ROW), F32),
        jax.ShapeDtypeStruct((n, WIN_ROW), F32),
        jax.ShapeDtypeStruct((7, n, HD), BF16),
        jax.ShapeDtypeStruct((n, MLA_ROW), BF16),
    )
    out_specs = (
        pl.BlockSpec((3, HEADS, tm, HD), lambda i: (0, 0, i, 0)),
        pl.BlockSpec((HEADS, tm, MLA_ROW), lambda i: (0, i, 0)),
        row(HEADS), row(3 * HEADS), row(DSA_ROW), row(NSA_ROW), row(MLA_ROW), row(WIN_ROW),
        pl.BlockSpec((7, tm, HD), lambda i: (0, i, 0)),
        row(MLA_ROW),
    )
    in_specs = [row(d), full((1, d)), full((d, ROPE_W)), full((d, PLAIN_W)), tab, tab, tab, tab, tab, tab,
                full((1, D_CQ)), full((D_CQ, HEADS * (D_NOPE + D_ROPE))), full((HEADS * D_NOPE, HEADS * D_C)),
                full((1, D_C))]
    return pl.pallas_call(
        _in_proj_body, grid=(n // tm,), in_specs=in_specs, out_specs=out_specs, out_shape=out_shape,
        compiler_params=_cparams(1), name="in_proj",
    )(x2d, pw['g1'], pw['wr'], pw['wp'], *tabs64, *tabs32, pw['gq'], pw['wuq'], pw['wuk'], pw['gkv'])


def _online_step(s, m, l):
    m_new = jnp.maximum(m, jnp.max(s, axis=1, keepdims=True))
    alpha = jnp.exp(m - m_new)
    p = jnp.exp(s - m_new)
    return m_new, alpha * l + jnp.sum(p, axis=1, keepdims=True), alpha, p.astype(BF16)


def _attn_init(m_rows, dv):
    return (jnp.full((m_rows, 1), 2 * NEG, F32), jnp.zeros((m_rows, 1), F32), jnp.zeros((m_rows, dv), F32))


def _attn_chunk(carry, q, k, v, bias, q_blk, scale=None):
    m, l, acc = carry
    width = k.shape[0]
    s = _nt_dot(q, k)
    if scale is not None:
        s = s * scale
    s = (s.reshape(HEADS, q_blk, width) + bias[None]).reshape(HEADS * q_blk, width)
    m, l, alpha, p = _online_step(s, m, l)
    return m, l, alpha * acc + _dot(p, v)


def _attn_result(carry):
    return carry[2] / carry[1]


def _heads_to_lanes(o, q):
    return jnp.concatenate([o[hh * q:(hh + 1) * q] for hh in range(HEADS)], axis=1)


def _count(mask):
    return jnp.sum(jnp.where(mask, 1.0, 0.0), axis=1, keepdims=True)


def _kth_largest_key(key_ref, k, width):
    rows = key_ref.shape[0]
    half = rows // 2
    kf = float(k)

    def part(r0, cand):
        acc = None
        for j in range(width // LANES):
            hit = jnp.where(key_ref[r0:r0 + half, j * LANES:(j + 1) * LANES] >= cand, 1.0, 0.0)
            acc = hit if acc is None else acc + hit
        return acc

    def settle(p, cand, base):
        return jnp.where(jnp.sum(p, axis=1, keepdims=True) >= kf, cand, base)

    zero = jnp.zeros((half, 1), jnp.int32)
    lowest = jnp.full((half, 1), INT_MIN, jnp.int32)
    base_a = settle(part(0, zero), zero, lowest)
    base_b = settle(part(half, zero), zero, lowest)

    def body(i, carry):
        base_a, base_b, part_b, cand_b = carry
        bit = lax.shift_left(jnp.int32(1), 30 - i)
        cand_a = base_a | bit
        part_a = part(0, cand_a)
        base_b = settle(part_b, cand_b, base_b)
        cand_b = base_b | bit
        part_b = part(half, cand_b)
        return settle(part_a, cand_a, base_a), base_b, part_b, cand_b

    base_a, base_b, part_b, cand_b = lax.fori_loop(
        0, 31, body, (base_a, base_b, jnp.zeros((half, LANES), F32), base_b))
    return jnp.concatenate([base_a, settle(part_b, cand_b, base_b)], axis=0)


def _prefix_upper(n):
    r = lax.broadcasted_iota(jnp.int32, (n, n), 0)
    c = lax.broadcasted_iota(jnp.int32, (n, n), 1)
    return jnp.where(r <= c, 1.0, 0.0).astype(BF16)


def _dsa_prompt_body(iq_ref, iw_ref, q_ref, ik_ref, k_ref, v_ref, o_ref, key_ref, bias_ref, thr_ref, *,
                     t_len, topk):
    q_blk = DSA_Q_BLK
    qi = pl.program_id(1)
    n_chunks = (qi * q_blk + q_blk + KV_CHUNK - 1) // KV_CHUNK
    t = qi * q_blk + lax.broadcasted_iota(jnp.int32, (q_blk, 1), 0)
    lane = lax.broadcasted_iota(jnp.int32, (1, KV_CHUNK), 1)
    iq = iq_ref[...].reshape(HEADS * q_blk, IDX_DIM)
    w = iw_ref[...] * (IDX_DIM ** -0.5)

    def idx_body(c, carry):
        ks = pl.multiple_of(c * KV_CHUNK, KV_CHUNK)
        s = jnp.maximum(_nt_dot(iq, ik_ref[pl.ds(ks, KV_CHUNK), :]), 0.0).reshape(HEADS, q_blk, KV_CHUNK)
        tot = w[:, 0:1] * s[0]
        for hh in range(1, HEADS):
            tot = tot + w[:, hh:hh + 1] * s[hh]
        key_ref[:, pl.ds(ks, KV_CHUNK)] = _sort_key(jnp.where(ks + lane <= t, tot, NEG))
        return carry

    lax.fori_loop(0, n_chunks, idx_body, 0)

    for nc in range(1, t_len // KV_CHUNK + 1):
        @pl.when(n_chunks == nc)
        def _(nc=nc):
            thr_ref[...] = _kth_largest_key(key_ref, topk, nc * KV_CHUNK)

    thr = thr_ref[...]

    def sel_body(c, carry):
        n_gt, n_eq = carry
        ks = pl.multiple_of(c * KV_CHUNK, KV_CHUNK)
        key = key_ref[:, pl.ds(ks, KV_CHUNK)]
        bias_ref[:, pl.ds(ks, KV_CHUNK)] = jnp.where(ks + lane <= t, jnp.where(key >= thr, 0.0, NEG), NEG)
        return n_gt + _count(key > thr), n_eq + _count(key == thr)

    zeros = jnp.zeros((q_blk, 1), F32)
    n_gt, n_eq = lax.fori_loop(0, n_chunks, sel_body, (zeros, zeros))
    overflow = jnp.where((n_gt + n_eq > float(topk)) & (thr > NEG_KEY), 1.0, 0.0)

    @pl.when(jnp.max(overflow) > 0.0)
    def _():
        pw = KV_CHUNK // 2
        upper = _prefix_upper(pw)
        quota = float(topk) - n_gt
        lane_p = lax.broadcasted_iota(jnp.int32, (1, pw), 1)

        def tie_body(c, used):
            ks = pl.multiple_of(c * pw, pw)
            kc = key_ref[:, pl.ds(ks, pw)]
            eq = kc == thr
            pre = _dot(jnp.where(eq, 1.0, 0.0).astype(BF16), upper) + used
            keep = jnp.where(eq, jnp.where(pre <= quota, 0.0, NEG), jnp.where(kc > thr, 0.0, NEG))
            bias_ref[:, pl.ds(ks, pw)] = jnp.where(ks + lane_p <= t, keep, NEG)
            return pre[:, pw - 1:pw]

        lax.fori_loop(0, n_chunks * (KV_CHUNK // pw), tie_body, zeros)

    q = q_ref[...].reshape(HEADS * q_blk, HD)

    def att_body(c, carry):
        ks = pl.multiple_of(c * KV_CHUNK, KV_CHUNK)
        return _attn_chunk(carry, q, k_ref[pl.ds(ks, KV_CHUNK), :], v_ref[pl.ds(ks, KV_CHUNK), :],
                           bias_ref[:, pl.ds(ks, KV_CHUNK)], q_blk)

    out = lax.fori_loop(0, n_chunks, att_body, _attn_init(HEADS * q_blk, HD))
    o_ref[...] = _heads_to_lanes(_attn_result(out), q_blk).astype(BF16)


def _dsa_prompt(q3, idx_w, kv7, bsz, t_len):
    n = bsz * t_len
    q_blk = DSA_Q_BLK
    nq = t_len // q_blk
    topk = min(DSA_TOPK, t_len // 4)
    assert t_len % q_blk == 0 and topk <= KV_CHUNK
    qspec = lambda g: pl.BlockSpec((None, HEADS, q_blk, HD), lambda b, i: (g, 0, b * nq + i, 0))
    kspec = lambda j: pl.BlockSpec((None, t_len, HD), lambda b, i: (j, b, 0))
    return pl.pallas_call(
        functools.partial(_dsa_prompt_body, t_len=t_len, topk=topk),
        grid=(bsz, nq),
        in_specs=[qspec(1), pl.BlockSpec((q_blk, HEADS), lambda b, i: (b * nq + i, 0)), qspec(0),
                  kspec(2), kspec(0), kspec(1)],
        out_specs=pl.BlockSpec((q_blk, HEADS * HD), lambda b, i: (b * nq + i, 0)),
        out_shape=jax.ShapeDtypeStruct((n, HEADS * HD), BF16),
        scratch_shapes=[pltpu.VMEM((q_blk, t_len), jnp.int32), pltpu.VMEM((q_blk, t_len), F32),
                        pltpu.VMEM((q_blk, 1), jnp.int32)],
        compiler_params=_cparams(2), name="dsa_prompt",
    )(q3, idx_w, q3, kv7, kv7, kv7)


def _mla_prompt_body(q_ref, k_ref, wuv_ref, o_ref):
    q_blk = Q_BLK
    qi = pl.program_id(1)
    n_chunks = (qi * q_blk + q_blk + KV_CHUNK - 1) // KV_CHUNK
    t = qi * q_blk + lax.broadcasted_iota(jnp.int32, (q_blk, 1), 0)
    lane = lax.broadcasted_iota(jnp.int32, (1, KV_CHUNK), 1)
    q = q_ref[...].reshape(HEADS * q_blk, MLA_ROW)

    def body(c, carry):
        ks = pl.multiple_of(c * KV_CHUNK, KV_CHUNK)
        rows = k_ref[pl.ds(ks, KV_CHUNK), :]
        return _attn_chunk(carry, q, rows, rows[:, :D_C], jnp.where(ks + lane <= t, 0.0, NEG), q_blk,
                           scale=(D_NOPE + D_ROPE) ** -0.5)

    lat = _attn_result(lax.fori_loop(0, n_chunks, body, _attn_init(HEADS * q_blk, D_C))).astype(BF16)
    o_ref[...] = jnp.concatenate(
        [_dot(lat[hh * q_blk:(hh + 1) * q_blk], wuv_ref[hh]) for hh in range(HEADS)], axis=1).astype(BF16)


def _mla_prompt(qcat, mla_bf, wuv, bsz, t_len):
    n = bsz * t_len
    nq = t_len // Q_BLK
    return pl.pallas_call(
        _mla_prompt_body, grid=(bsz, nq),
        in_specs=[pl.BlockSpec((HEADS, Q_BLK, MLA_ROW), lambda b, i: (0, b * nq + i, 0)),
                  pl.BlockSpec((t_len, MLA_ROW), lambda b, i: (b, 0)),
                  pl.BlockSpec((HEADS, D_C, D_V), lambda b, i: (0, 0, 0))],
        out_specs=pl.BlockSpec((Q_BLK, HEADS * D_V), lambda b, i: (b * nq + i, 0)),
        out_shape=jax.ShapeDtypeStruct((n, HEADS * D_V), BF16),
        compiler_params=_cparams(2), name="mla_prompt",
    )(qcat, mla_bf, wuv)


def _compress_blocks(read_rows, pos_ref, w1_ref, w2_ref, n_blk):
    acc = jnp.zeros((n_blk, 2 * CMP_HID), F32)
    for r in range(CMP_BLK):
        xr = (read_rows(r) + pos_ref[r:r + 1, :]).astype(BF16)
        acc = acc + _dot(xr, w1_ref[r])
    hid = jax.nn.gelu(acc).astype(BF16)
    return _dot(hid, w2_ref[...])


def _cmp_attend(q, ckv, vis):
    s = _nt_dot(q, ckv[:, :HD].astype(BF16))
    s = jnp.where(vis, s, NEG)
    e = jnp.exp(s - jnp.max(s, axis=1, keepdims=True))
    p = jnp.where(vis, e / jnp.sum(e, axis=1, keepdims=True), 0.0)
    return _dot(p.astype(BF16), ckv[:, HD:].astype(BF16)), p


def _nsa_prompt_body(q_ref, g_ref, rows_ref, sk_ref, sv_ref, wk_ref, wv_ref, pos_ref, w1_ref, w2_ref,
                     o_ref, ckv_ref, exp_ref, bias_ref, *, t_len):
    q_blk = Q_BLK
    n_cmp = t_len // CMP_BLK
    n_sel_blk = t_len // SEL_BLK
    ratio = SEL_BLK // CMP_BLK
    n_keep = min(N_SEL, n_sel_blk)
    qi = pl.program_id(1)
    t = qi * q_blk + lax.broadcasted_iota(jnp.int32, (q_blk, 1), 0)

    @pl.when(qi == 0)
    def _():
        ckv_ref[...] = _compress_blocks(lambda r: rows_ref[pl.ds(r, n_cmp, stride=CMP_BLK), :],
                                        pos_ref, w1_ref, w2_ref, n_cmp)
        j = lax.broadcasted_iota(jnp.int32, (n_cmp, t_len), 0)
        s = lax.broadcasted_iota(jnp.int32, (n_cmp, t_len), 1)
        exp_ref[...] = jnp.where(j == (s // SEL_BLK) * ratio, 1.0, 0.0).astype(BF16)

    q = q_ref[...].reshape(HEADS * q_blk, HD)
    g = g_ref[...]

    jc = lax.broadcasted_iota(jnp.int32, (1, n_cmp), 1)
    vis = (jc + 1) * CMP_BLK - 1 <= t
    vis8 = jnp.broadcast_to(vis[None], (HEADS, q_blk, n_cmp)).reshape(HEADS * q_blk, n_cmp)
    o_c, p = _cmp_attend(q, ckv_ref[...], vis8)
    imp = p[0:q_blk]
    for hh in range(1, HEADS):
        imp = imp + p[hh * q_blk:(hh + 1) * q_blk]
    assert ratio == 2
    width = -(-n_cmp // LANES) * LANES
    if width > n_cmp:
        imp = jnp.concatenate([imp, jnp.zeros((q_blk, width - n_cmp), F32)], axis=1)
    val = imp + pltpu.roll(imp, width - 1, 1)
    lw = lax.broadcasted_iota(jnp.int32, (1, width), 1)
    jb = lw // ratio
    cur = t // SEL_BLK
    val = jnp.where((jb == cur) | (jb == 0), FORCE, val)
    val = jnp.where(jb > cur, NEG, val)
    valid = (lw % ratio == 0) & (lw < n_cmp)
    val = jnp.where(valid, val, -jnp.inf)
    rank = jnp.zeros((q_blk, width), F32)
    for i in range(0, n_cmp, ratio):
        col = val[:, i:i + 1]
        rank = rank + jnp.where(lw > i, jnp.where(col >= val, 1.0, 0.0), jnp.where(col > val, 1.0, 0.0))
    selmask = jnp.where(valid & (rank < float(n_keep)), 1.0, 0.0)[:, :n_cmp].astype(BF16)
    n_chunks = (qi * q_blk + q_blk + KV_CHUNK - 1) // KV_CHUNK
    lane = lax.broadcasted_iota(jnp.int32, (1, KV_CHUNK), 1)

    def bias_body(c, carry):
        ks = pl.multiple_of(c * KV_CHUNK, KV_CHUNK)
        kept = _dot(selmask, exp_ref[:, pl.ds(ks, KV_CHUNK)]) > 0.5
        bias_ref[:, pl.ds(ks, KV_CHUNK)] = jnp.where(ks + lane <= t, jnp.where(kept, 0.0, NEG), NEG)
        return carry

    lax.fori_loop(0, n_chunks, bias_body, 0)

    def sel_body(c, carry):
        ks = pl.multiple_of(c * KV_CHUNK, KV_CHUNK)
        return _attn_chunk(carry, q, sk_ref[pl.ds(ks, KV_CHUNK), :], sv_ref[pl.ds(ks, KV_CHUNK), :],
                           bias_ref[:, pl.ds(ks, KV_CHUNK)], q_blk)

    o_s = _attn_result(lax.fori_loop(0, n_chunks, sel_body, _attn_init(HEADS * q_blk, HD)))

    wk_len = min(WINDOW + q_blk, t_len)
    start = pl.multiple_of(jnp.clip(qi * q_blk - WINDOW, 0, t_len - wk_len), LANES)
    kpos = start + lax.broadcasted_iota(jnp.int32, (1, wk_len), 1)
    o_w = _attn_result(_attn_chunk(
        _attn_init(HEADS * q_blk, HD), q, wk_ref[pl.ds(start, wk_len), :], wv_ref[pl.ds(start, wk_len), :],
        jnp.where(kpos <= t, jnp.where(kpos >= t - WINDOW, 0.0, NEG), NEG), q_blk))

    outs = []
    for hh in range(HEADS):
        r = slice(hh * q_blk, (hh + 1) * q_blk)
        outs.append(g[:, hh:hh + 1] * o_c[r] + g[:, HEADS + hh:HEADS + hh + 1] * o_s[r]
                    + g[:, 2 * HEADS + hh:2 * HEADS + hh + 1] * o_w[r])
    o_ref[...] = jnp.concatenate(outs, axis=1).astype(BF16)


def _nsa_prompt(q3, nsa_g, rows_nsa, kv7, pw, bsz, t_len):
    n = bsz * t_len
    nq = t_len // Q_BLK
    n_cmp = t_len // CMP_BLK
    kspec = lambda j: pl.BlockSpec((None, t_len, HD), lambda b, i: (j, b, 0))
    full = lambda shape: pl.BlockSpec(shape, lambda b, i: (0,) * len(shape))
    return pl.pallas_call(
        functools.partial(_nsa_prompt_body, t_len=t_len),
        grid=(bsz, nq),
        in_specs=[pl.BlockSpec((None, HEADS, Q_BLK, HD), lambda b, i: (2, 0, b * nq + i, 0)),
                  pl.BlockSpec((Q_BLK, 3 * HEADS), lambda b, i: (b * nq + i, 0)),
                  pl.BlockSpec((t_len, 2 * HD), lambda b, i: (b, 0)),
                  kspec(3), kspec(4), kspec(5), kspec(6),
                  full((CMP_BLK, 2 * HD)), full((CMP_BLK, 2 * HD, 2 * CMP_HID)), full((2 * CMP_HID, 2 * HD))],
        out_specs=pl.BlockSpec((Q_BLK, HEADS * HD), lambda b, i: (b * nq + i, 0)),
        out_shape=jax.ShapeDtypeStruct((n, HEADS * HD), BF16),
        scratch_shapes=[pltpu.VMEM((n_cmp, 2 * HD), F32), pltpu.VMEM((n_cmp, t_len), BF16),
                        pltpu.VMEM((Q_BLK, t_len), F32)],
        compiler_params=_cparams(2), name="nsa_prompt",
    )(q3, nsa_g, rows_nsa, kv7, kv7, kv7, kv7, pw['cmp_pos'], pw['w1bd'], pw['w2bd'])


def _merge_out_body(x_ref, g1_ref, wmg_ref, od_ref, on_ref, om_ref, wbr_ref, wo_ref, x1_ref):
    x = x_ref[...]
    d = x.shape[1]
    h = (x * lax.rsqrt(jnp.mean(x * x, axis=-1, keepdims=True) + EPS) * g1_ref[...]).astype(BF16)
    gate = jax.nn.sigmoid(_dot(h, wmg_ref[...]))
    mix = (gate[:, 0:d] * _dot(od_ref[...], wbr_ref[0]) + gate[:, d:2 * d] * _dot(on_ref[...], wbr_ref[1])
           + gate[:, 2 * d:3 * d] * _dot(om_ref[...], wbr_ref[2]))
    x1_ref[...] = x + _dot(mix.astype(BF16), wo_ref[...])


def _merge_out(x2d, o_dsa, o_nsa, o_mla, pw, tm):
    n, d = x2d.shape
    row = lambda w: pl.BlockSpec((tm, w), lambda i: (i, 0))
    full = lambda shape: pl.BlockSpec(shape, lambda i: (0,) * len(shape))
    return pl.pallas_call(
        _merge_out_body, grid=(n // tm,),
        in_specs=[row(d), full((1, d)), full((d, 3 * d)), row(HEADS * HD), row(HEADS * HD), row(HEADS * D_V),
                  full((3, HEADS * HD, d)), full((d, d))],
        out_specs=row(d), out_shape=jax.ShapeDtypeStruct((n, d), F32),
        compiler_params=_cparams(1), name="merge_out",
    )(x2d, pw['g1'], pw['wmg'], o_dsa, o_nsa, o_mla, pw['wbr'], pw['wo'])


def _ffn_body(x_ref, g2_ref, wg_ref, wu_ref, wd_ref, gf_ref, x2_ref, *maybe_y_ref):
    x = x_ref[...]
    h = (x * lax.rsqrt(jnp.mean(x * x, axis=-1, keepdims=True) + EPS) * g2_ref[...]).astype(BF16)
    act = (jax.nn.silu(_dot(h, wg_ref[...])) * _dot(h, wu_ref[...])).astype(BF16)
    x2 = x + _dot(act, wd_ref[...])
    x2_ref[...] = x2
    for y_ref in maybe_y_ref:
        y_ref[...] = x2 * lax.rsqrt(jnp.mean(x2 * x2, axis=-1, keepdims=True) + EPS) * gf_ref[...]


def _ffn(x1, pw, gf, tm, last):
    n, d = x1.shape
    dff = pw['wg'].shape[1]
    row = pl.BlockSpec((tm, d), lambda i: (i, 0))
    full = lambda shape: pl.BlockSpec(shape, lambda i: (0,) * len(shape))
    n_out = 2 if last else 1
    outs = pl.pallas_call(
        _ffn_body, grid=(n // tm,),
        in_specs=[row, full((1, d)), full((d, dff)), full((d, dff)), full((dff, d)), full((1, d))],
        out_specs=(row,) * n_out,
        out_shape=(jax.ShapeDtypeStruct((n, d), F32),) * n_out,
        compiler_params=_cparams(1), name="ffn",
    )(x1, pw['g2'], pw['wg'], pw['wu'], pw['wd'], gf)
    return (outs[0], outs[1]) if last else (outs[0], None)


def _page_buffer(layout, ch, width):
    return pltpu.VMEM({"features": (2, width, ch), "blocks": (2, ch // CMP_BLK, BLK_PITCH, width)}[layout], F32)


def _page_copy(pool_ref, layer, page, lo, width, buf_ref, slot, j, sem_ref, layout):
    if layout == "features":
        return pltpu.make_async_copy(pool_ref.at[layer, page, pl.ds(lo, width), :],
                                     buf_ref.at[slot, :, pl.ds(j * PAGE_SIZE, PAGE_SIZE)], sem_ref.at[slot])
    assert layout == "blocks"
    per = PAGE_SIZE // CMP_BLK
    return pltpu.make_async_copy(pool_ref.at[layer, page, :, :, pl.ds(lo, width)],
                                 buf_ref.at[slot, pl.ds(j * per, per), pl.ds(0, CMP_BLK), :], sem_ref.at[slot])


def _stream_chunk(pt_ref, pool_ref, buf_ref, sem_ref, *, layer, lo, width, ppc, layout):
    b, c = pl.program_id(0), pl.program_id(1)
    nb, nc = pl.num_programs(0), pl.num_programs(1)
    step = b * nc + c
    slot = step % 2

    def start(bb, cc, sl):
        for j in range(ppc):
            _page_copy(pool_ref, layer, pt_ref[bb, cc * ppc + j], lo, width, buf_ref, sl, j, sem_ref,
                       layout).start()

    @pl.when(step == 0)
    def _():
        start(b, c, slot)

    @pl.when(step + 1 < nb * nc)
    def _():
        last = c + 1 == nc
        start(jnp.where(last, b + 1, b), jnp.where(last, 0, c + 1), 1 - slot)

    for j in range(ppc):
        _page_copy(pool_ref, layer, 0, lo, width, buf_ref, slot, j, sem_ref, layout).wait()
    return slot


def _pages_per_chunk(n_pages):
    ppc = min(64, n_pages)
    assert n_pages % ppc == 0
    return ppc


def _stream_call(body, pt, in_arrays, in_specs, pool, out_shape, out_specs, buf_width, scratch, name,
                 layout):
    bsz, n_pages = pt.shape
    ppc = _pages_per_chunk(n_pages)
    ch = ppc * PAGE_SIZE
    grid_spec = pltpu.PrefetchScalarGridSpec(
        num_scalar_prefetch=1, grid=(bsz, n_pages // ppc),
        in_specs=list(in_specs) + [pl.BlockSpec(memory_space=pl.ANY)],
        out_specs=out_specs,
        scratch_shapes=[_page_buffer(layout, ch, buf_width), pltpu.SemaphoreType.DMA((2,))] + list(scratch))
    return pl.pallas_call(body, grid_spec=grid_spec, out_shape=out_shape, compiler_params=_cparams(2),
                          name=name)(pt, *in_arrays, pool)


def _dsa_scores_body(pt_ref, iq_ref, iw_ref, newk_ref, pool_ref, sc_ref, snew_ref, buf_ref, sem_ref, *, layer, ppc):
    slot = _stream_chunk(pt_ref, pool_ref, buf_ref, sem_ref, layer=layer, lo=2 * HD, width=IDX_DIM, ppc=ppc,
                         layout="features")
    iq = iq_ref[...]
    w = iw_ref[...] * (IDX_DIM ** -0.5)
    s = jnp.maximum(_dot(iq, buf_ref[slot].astype(BF16)), 0.0) * w
    sc_ref[...] = jnp.sum(s, axis=0, keepdims=True)

    @pl.when(pl.program_id(1) == pl.num_programs(1) - 1)
    def _():
        kn = newk_ref[...].astype(BF16).astype(F32)
        sn = jnp.maximum(jnp.sum(iq.astype(F32) * kn, axis=1, keepdims=True), 0.0) * w
        sn = jnp.sum(sn, axis=0, keepdims=True)
        lane = lax.broadcasted_iota(jnp.int32, (1, LANES), 1)
        snew_ref[...] = jnp.where(lane == 0, sn, NEG)


def _dsa_scores(pt, idx_q_s, idx_w_s, idx_k_new, cache, layer):
    bsz, n_pages = pt.shape
    ppc = _pages_per_chunk(n_pages)
    ch = ppc * PAGE_SIZE
    return _stream_call(
        functools.partial(_dsa_scores_body, layer=layer, ppc=ppc), pt,
        (idx_q_s, idx_w_s, idx_k_new),
        [pl.BlockSpec((None, HEADS, IDX_DIM), lambda b, c, pt: (b, 0, 0)),
         pl.BlockSpec((None, HEADS, 1), lambda b, c, pt: (b, 0, 0)),
         pl.BlockSpec((None, 1, IDX_DIM), lambda b, c, pt: (b, 0, 0))],
        cache,
        (jax.ShapeDtypeStruct((bsz, 1, n_pages * PAGE_SIZE), F32), jax.ShapeDtypeStruct((bsz, 1, LANES), F32)),
        (pl.BlockSpec((None, 1, ch), lambda b, c, pt: (b, 0, c)),
         pl.BlockSpec((None, 1, LANES), lambda b, c, pt: (b, 0, 0))),
        IDX_DIM, [], "dsa_scores", "features")


def _dsa_thresh_body(sc_ref, snew_ref, thr_ref, quota_ref, key_ref, *, topk):
    past = sc_ref.shape[1]
    key_ref[:, 0:past] = _sort_key(sc_ref[...])
    key_ref[:, past:past + LANES] = _sort_key(snew_ref[...])
    thr = _kth_largest_key(key_ref, topk, past + LANES)
    key = key_ref[...]
    n_gt = _count(key > thr)
    n_eq = _count(key == thr)
    bits = thr ^ ((thr >> 31) & 0x7FFFFFFF)
    thr_ref[...] = jnp.broadcast_to(pltpu.bitcast(bits, F32), thr_ref.shape)
    quota = jnp.where(n_gt + n_eq > float(topk), float(topk) - n_gt, float(2 ** 24))
    quota_ref[...] = jnp.broadcast_to(quota, quota_ref.shape)


def _dsa_thresh(scores, snew, topk):
    bsz, past = scores.shape
    return pl.pallas_call(
        functools.partial(_dsa_thresh_body, topk=topk),
        out_shape=(jax.ShapeDtypeStruct((bsz, LANES), F32), jax.ShapeDtypeStruct((bsz, LANES), F32)),
        scratch_shapes=[pltpu.VMEM((bsz, past + LANES), jnp.int32)],
        compiler_params=pltpu.CompilerParams(vmem_limit_bytes=V7X_VMEM_LIMIT), name="dsa_thresh",
    )(scores, snew)


def _dsa_attend_body(pt_ref, q_ref, sc_ref, snew_ref, thr_ref, quota_ref, new_ref, pool_ref, o_ref,
                     buf_ref, sem_ref, m_ref, l_ref, acc_ref, used_ref, bias_ref, *, layer, ppc):
    slot = _stream_chunk(pt_ref, pool_ref, buf_ref, sem_ref, layer=layer, lo=0, width=2 * HD, ppc=ppc,
                         layout="features")
    c = pl.program_id(1)
    ch = ppc * PAGE_SIZE

    @pl.when(c == 0)
    def _():
        m_ref[...] = jnp.full(m_ref.shape, 2 * NEG, F32)
        l_ref[...] = jnp.zeros(l_ref.shape, F32)
        acc_ref[...] = jnp.zeros(acc_ref.shape, F32)
        used_ref[...] = jnp.zeros(used_ref.shape, F32)

    thr = thr_ref[:, 0:1]
    quota = quota_ref[:, 0:1]
    sc = sc_ref[...] + 0.0
    bias_ref[...] = jnp.where(sc >= thr, 0.0, NEG)

    @pl.when(jnp.max(quota_ref[...]) < float(2 ** 23))
    def _():
        pw = 512
        upper = _prefix_upper(pw)
        used = used_ref[...]
        for i in range(ch // pw):
            sci = sc[:, i * pw:(i + 1) * pw]
            eq = sci == thr
            e8 = jnp.broadcast_to(jnp.where(eq, 1.0, 0.0), (8, pw)).astype(BF16)
            pre = _dot(e8, upper)[0:1] + used
            bias_ref[:, i * pw:(i + 1) * pw] = jnp.where((sci > thr) | (eq & (pre <= quota)), 0.0, NEG)
            used = pre[:, pw - 1:pw]
        used_ref[...] = used

    q = q_ref[...]
    kv = buf_ref[slot].astype(BF16)
    s = _dot(q, kv[0:HD]) + bias_ref[...]
    m, l, alpha, p = _online_step(s, m_ref[...], l_ref[...])
    acc = alpha * acc_ref[...] + _nt_dot(p, kv[HD:2 * HD])
    m_ref[...], l_ref[...], acc_ref[...] = m, l, acc

    @pl.when(c == pl.num_programs(1) - 1)
    def _():
        new = new_ref[...].astype(BF16).astype(F32)
        sn = snew_ref[:, 0:1] + 0.0
        sel_n = (sn > thr) | ((sn == thr) & (used_ref[...] + 1.0 <= quota))
        s_n = jnp.where(sel_n, jnp.sum(q.astype(F32) * new[:, :HD], axis=1, keepdims=True), NEG)
        m_new = jnp.maximum(m, s_n)
        alpha = jnp.exp(m - m_new)
        p_n = jnp.exp(s_n - m_new).astype(BF16).astype(F32)
        o = (alpha * acc + p_n * new[:, HD:2 * HD]) / (alpha * l + p_n)
        o_ref[...] = o.astype(BF16)


def _dsa_attend(pt, dsa_q_s, scores, snew, thr, quota, rows_new, cache, layer):
    bsz, n_pages = pt.shape
    ppc = _pages_per_chunk(n_pages)
    ch = ppc * PAGE_SIZE
    per_b = lambda w: pl.BlockSpec((None, 1, w), lambda b, c, pt: (b, 0, 0))
    return _stream_call(
        functools.partial(_dsa_attend_body, layer=layer, ppc=ppc), pt,
        (dsa_q_s, scores, snew, thr, quota, rows_new),
        [pl.BlockSpec((None, HEADS, HD), lambda b, c, pt: (b, 0, 0)),
         pl.BlockSpec((None, 1, ch), lambda b, c, pt: (b, 0, c)),
         per_b(LANES), per_b(LANES), per_b(LANES), per_b(DSA_ROW)],
        cache,
        jax.ShapeDtypeStruct((bsz, HEADS, HD), BF16),
        pl.BlockSpec((None, HEADS, HD), lambda b, c, pt: (b, 0, 0)),
        2 * HD,
        [pltpu.VMEM((HEADS, 1), F32), pltpu.VMEM((HEADS, 1), F32), pltpu.VMEM((HEADS, HD), F32),
         pltpu.VMEM((1, 1), F32), pltpu.VMEM((1, ch), F32)],
        "dsa_attend", "features")


def _mla_decode_body(pt_ref, q_ref, new_ref, wuv_ref, pool_ref, o_ref, buf_ref, sem_ref, m_ref, l_ref, acc_ref,
                     *, layer, ppc):
    slot = _stream_chunk(pt_ref, pool_ref, buf_ref, sem_ref, layer=layer, lo=0, width=MLA_ROW, ppc=ppc,
                         layout="features")
    c = pl.program_id(1)
    scale = (D_NOPE + D_ROPE) ** -0.5

    @pl.when(c == 0)
    def _():
        m_ref[...] = jnp.full(m_ref.shape, 2 * NEG, F32)
        l_ref[...] = jnp.zeros(l_ref.shape, F32)
        acc_ref[...] = jnp.zeros(acc_ref.shape, F32)

    q = q_ref[...]
    rows = buf_ref[slot].astype(BF16)
    s = _dot(q, rows) * scale
    m, l, alpha, p = _online_step(s, m_ref[...], l_ref[...])
    acc = alpha * acc_ref[...] + _nt_dot(p, rows[0:D_C])
    m_ref[...], l_ref[...], acc_ref[...] = m, l, acc

    @pl.when(c == pl.num_programs(1) - 1)
    def _():
        new = new_ref[...].astype(BF16).astype(F32)
        s_n = jnp.sum(q.astype(F32) * new, axis=1, keepdims=True) * scale
        m_new = jnp.maximum(m, s_n)
        alpha = jnp.exp(m - m_new)
        p_n = jnp.exp(s_n - m_new).astype(BF16).astype(F32)
        lat = ((alpha * acc + p_n * new[:, :D_C]) / (alpha * l + p_n)).astype(BF16)
        full = _dot(lat, wuv_ref[...])
        hrow = lax.broadcasted_iota(jnp.int32, full.shape, 0)
        hcol = lax.broadcasted_iota(jnp.int32, full.shape, 1) // D_V
        o_ref[...] = jnp.sum(jnp.where(hrow == hcol, full, 0.0), axis=0, keepdims=True).astype(BF16)


def _mla_decode(pt, qcat_s, rows_new, wuv_flat, cache, layer):
    bsz, n_pages = pt.shape
    ppc = _pages_per_chunk(n_pages)
    return _stream_call(
        functools.partial(_mla_decode_body, layer=layer, ppc=ppc), pt,
        (qcat_s, rows_new, wuv_flat),
        [pl.BlockSpec((None, HEADS, MLA_ROW), lambda b, c, pt: (b, 0, 0)),
         pl.BlockSpec((None, 1, MLA_ROW), lambda b, c, pt: (b, 0, 0)),
         pl.BlockSpec((D_C, HEADS * D_V), lambda b, c, pt: (0, 0))],
        cache,
        jax.ShapeDtypeStruct((bsz, 1, HEADS * D_V), BF16),
        pl.BlockSpec((None, 1, HEADS * D_V), lambda b, c, pt: (b, 0, 0)),
        MLA_ROW,
        [pltpu.VMEM((HEADS, 1), F32), pltpu.VMEM((HEADS, 1), F32), pltpu.VMEM((HEADS, D_C), F32)],
        "mla_decode", "features")


def _nsa_cmp_body(pt_ref, q_ref, pos_ref, w1_ref, w2_ref, pool_ref, oc_ref, imp_ref, buf_ref, sem_ref, ckv_ref,
                  *, layer, ppc):
    slot = _stream_chunk(pt_ref, pool_ref, buf_ref, sem_ref, layer=layer, lo=0, width=2 * HD, ppc=ppc,
                         layout="blocks")
    c = pl.program_id(1)
    n_blk = ppc * PAGE_SIZE // CMP_BLK
    flat = buf_ref.at[slot].reshape(n_blk * BLK_PITCH, 2 * HD)
    out = _compress_blocks(lambda r: flat[pl.ds(r, n_blk, stride=BLK_PITCH), :], pos_ref, w1_ref, w2_ref, n_blk)
    ckv_ref[pl.ds(pl.multiple_of(c * n_blk, n_blk), n_blk), :] = out

    @pl.when(c == pl.num_programs(1) - 1)
    def _():
        n_cmp = ckv_ref.shape[0]
        vis = jnp.full((HEADS, n_cmp), True)
        o_c, p = _cmp_attend(q_ref[...], ckv_ref[...], vis)
        oc_ref[...] = o_c
        imp = jnp.sum(p, axis=0, keepdims=True)
        imp_ref[...] = imp + pltpu.roll(imp, n_cmp - 1, 1)


def _nsa_cmp(pt, nsa_q_s, pw, cache, layer):
    bsz, n_pages = pt.shape
    ppc = _pages_per_chunk(n_pages)
    n_cmp = n_pages * PAGE_SIZE // CMP_BLK
    full = lambda shape: pl.BlockSpec(shape, lambda b, c, pt: (0,) * len(shape))
    return _stream_call(
        functools.partial(_nsa_cmp_body, layer=layer, ppc=ppc), pt,
        (nsa_q_s, pw['cmp_pos'], pw['w1bd'], pw['w2bd']),
        [pl.BlockSpec((None, HEADS, HD), lambda b, c, pt: (b, 0, 0)),
         full((CMP_BLK, 2 * HD)), full((CMP_BLK, 2 * HD, 2 * CMP_HID)), full((2 * CMP_HID, 2 * HD))],
        cache,
        (jax.ShapeDtypeStruct((bsz, HEADS, HD), F32), jax.ShapeDtypeStruct((bsz, 1, n_cmp), F32)),
        (pl.BlockSpec((None, HEADS, HD), lambda b, c, pt: (b, 0, 0)),
         pl.BlockSpec((None, 1, n_cmp), lambda b, c, pt: (b, 0, 0))),
        2 * HD, [pltpu.VMEM((n_cmp, 2 * HD), F32)], "nsa_cmp", "blocks")


def _nsa_pick_body(imp_ref, ids_ref, *, n_pick):
    imp = imp_ref[...]
    width = imp.shape[1]
    ratio = SEL_BLK // CMP_BLK
    lane = lax.broadcasted_iota(jnp.int32, imp.shape, 1)
    val = jnp.where((lane % ratio == 0) & (lane >= ratio), imp, -jnp.inf)
    slot = lax.broadcasted_iota(jnp.int32, ids_ref.shape, 1)
    ids = jnp.zeros(ids_ref.shape, jnp.int32)
    for i in range(n_pick):
        m = jnp.max(val, axis=1, keepdims=True)
        idx = jnp.min(jnp.where(val == m, lane, width), axis=1, keepdims=True)
        ids = jnp.where(slot == i + 1, idx // ratio, ids)
        val = jnp.where(lane == idx, -jnp.inf, val)
    ids_ref[...] = ids


def _nsa_pick(imp, n_pick):
    bsz = imp.shape[0]
    return pl.pallas_call(
        functools.partial(_nsa_pick_body, n_pick=n_pick),
        out_shape=jax.ShapeDtypeStruct((bsz, LANES), jnp.int32), name="nsa_pick",
    )(imp)


def _nsa_sel_body(pt_ref, ids_ref, q_ref, g_ref, oc_ref, new_ref, newwin_ref, win_ref, pool_ref,
                  o_ref, wout_ref, buf_ref, sem_ref, *, layer, n_blk):
    b = pl.program_id(0)
    nb = pl.num_programs(0)
    slot = b % 2
    per_page = PAGE_SIZE // SEL_BLK

    def copy(bb, i, sl):
        blk = ids_ref[bb, i]
        page = pt_ref[bb, blk // per_page]
        return pltpu.make_async_copy(
            pool_ref.at[layer, page, pl.ds((blk % per_page) * SEL_BLK, SEL_BLK), pl.ds(2 * HD, 2 * HD)],
            buf_ref.at[sl, pl.ds(i * SEL_BLK, SEL_BLK), :], sem_ref.at[sl])

    def start(bb, sl):
        for i in range(n_blk):
            copy(bb, i, sl).start()

    @pl.when(b == 0)
    def _():
        start(b, slot)

    @pl.when(b + 1 < nb)
    def _():
        start(b + 1, 1 - slot)

    for i in range(n_blk):
        copy(b, i, slot).wait()

    q = q_ref[...]
    qf = q.astype(F32)

    def attend(rows, new):
        rows = rows.astype(BF16)
        new = new.astype(BF16).astype(F32)
        s = _nt_dot(q, rows[:, :HD])
        s_n = jnp.sum(qf * new[:, :HD], axis=1, keepdims=True)
        m = jnp.maximum(jnp.max(s, axis=1, keepdims=True), s_n)
        e = jnp.exp(s - m)
        e_n = jnp.exp(s_n - m)
        num = _dot(e.astype(BF16), rows[:, HD:]) + e_n.astype(BF16).astype(F32) * new[:, HD:]
        return num / (jnp.sum(e, axis=1, keepdims=True) + e_n)

    new = new_ref[...]
    o_s = attend(buf_ref[slot], new[:, 2 * HD:])
    win = win_ref[...]
    newwin = newwin_ref[...]
    o_w = attend(win, newwin)
    g = g_ref[...]
    o_ref[...] = (g[:, 0:1] * oc_ref[...] + g[:, 1:2] * o_s + g[:, 2:3] * o_w).astype(BF16)
    w = win.shape[0]
    wout_ref[0:w - 1, :] = win[1:w, :]
    wout_ref[w - 1:w, :] = newwin


def _nsa_sel(pt, ids, nsa_q_s, gates_s, o_c, rows_new, win_new, win_buf, cache, layer, n_blk):
    bsz = pt.shape[0]
    w = win_buf.shape[1]
    per_b = lambda r, c: pl.BlockSpec((None, r, c), lambda b, pt, ids: (b, 0, 0))
    grid_spec = pltpu.PrefetchScalarGridSpec(
        num_scalar_prefetch=2, grid=(bsz,),
        in_specs=[per_b(HEADS, HD), per_b(HEADS, 3), per_b(HEADS, HD), per_b(1, NSA_ROW), per_b(1, WIN_ROW),
                  per_b(w, WIN_ROW), pl.BlockSpec(memory_space=pl.ANY)],
        out_specs=(per_b(HEADS, HD), per_b(w, WIN_ROW)),
        scratch_shapes=[pltpu.VMEM((2, n_blk * SEL_BLK, 2 * HD), F32), pltpu.SemaphoreType.DMA((2,))])
    return pl.pallas_call(
        functools.partial(_nsa_sel_body, layer=layer, n_blk=n_blk), grid_spec=grid_spec,
        out_shape=(jax.ShapeDtypeStruct((bsz, HEADS, HD), BF16), jax.ShapeDtypeStruct((bsz, w, WIN_ROW), F32)),
        compiler_params=_cparams(1), name="nsa_sel",
    )(pt, ids, nsa_q_s, gates_s, o_c, rows_new, win_new, win_buf, cache)


def _layer_prompt(x2d, pw, gf, tabs64, tabs32, bsz, t_len, last):
    tm = 256
    (q3, qcat, idx_w, nsa_g, rows_dsa, rows_nsa, rows_mla, win, kv7, mla_bf) = _in_proj(
        x2d, pw, tabs64, tabs32, tm, t_len // tm)
    o_dsa = _dsa_prompt(q3, idx_w, kv7, bsz, t_len)
    o_nsa = _nsa_prompt(q3, nsa_g, rows_nsa, kv7, pw, bsz, t_len)
    o_mla = _mla_prompt(qcat, mla_bf, pw['wuv'], bsz, t_len)
    x1 = _merge_out(x2d, o_dsa, o_nsa, o_mla, pw, tm)
    x2, y = _ffn(x1, pw, gf, tm, last)
    return x2, y, rows_dsa, rows_nsa, rows_mla, win


def _layer_sample(x2d, pw, gf, tabs64, tabs32, layer, cache_dsa, cache_nsa, cache_mla, win_buf, pt, last):
    bsz = x2d.shape[0]
    n_pages = pt.shape[1]
    past = n_pages * PAGE_SIZE
    (q3, qcat, idx_w, nsa_g, rows_dsa, rows_nsa, rows_mla, win, _, _) = _in_proj(x2d, pw, tabs64, tabs32, bsz, 1)
    heads_first = lambda a: jnp.transpose(a, (1, 0, 2))
    dsa_q, idx_q, nsa_q = heads_first(q3[0]), heads_first(q3[1]), heads_first(q3[2])
    qcat_s = heads_first(qcat)
    scores, snew = _dsa_scores(pt, idx_q, idx_w[:, :, None], rows_dsa[:, None, 2 * HD:], cache_dsa, layer)
    topk = min(DSA_TOPK, (past + 1) // 4)
    thr, quota = _dsa_thresh(scores[:, 0, :], snew[:, 0, :], topk)
    o_dsa = _dsa_attend(pt, dsa_q, scores, snew, thr[:, None, :], quota[:, None, :], rows_dsa[:, None, :],
                        cache_dsa, layer)
    n_past_blk = past // SEL_BLK
    n_pick = min(N_SEL, n_past_blk + 1) - 2
    assert past % SEL_BLK == 0 and 0 <= n_pick <= n_past_blk - 1
    depth, n_pool = cache_nsa.shape[:2]
    cache_blocks = cache_nsa.reshape(depth, n_pool, PAGE_SIZE // CMP_BLK, CMP_BLK, NSA_ROW)
    o_c, imp = _nsa_cmp(pt, nsa_q, pw, cache_blocks, layer)
    ids = _nsa_pick(imp[:, 0, :], n_pick)[:, :n_pick + 1]
    gates = jnp.transpose(nsa_g.reshape(bsz, 3, HEADS), (0, 2, 1))
    o_nsa, win_out = _nsa_sel(pt, ids, nsa_q, gates, o_c, rows_nsa[:, None, :], win[:, None, :], win_buf,
                              cache_nsa, layer, n_pick + 1)
    wuv_flat = jnp.transpose(pw['wuv'], (1, 0, 2)).reshape(D_C, HEADS * D_V)
    o_mla = _mla_decode(pt, qcat_s, rows_mla[:, None, :], wuv_flat, cache_mla, layer)
    x1 = _merge_out(x2d, o_dsa.reshape(bsz, HEADS * HD), o_nsa.reshape(bsz, HEADS * HD),
                    o_mla.reshape(bsz, HEADS * D_V), pw, bsz)
    x2, y = _ffn(x1, pw, gf, bsz, last)
    return x2, y, rows_dsa, rows_nsa, rows_mla, win_out


def kernel(x_prompt, x_sample, cache_dsa, cache_nsa, cache_mla, state_nsa_win, page_table, norm1_g, w_in,
           mla_q_norm_g, w_uq, mla_kv_norm_g, w_uk, w_uv, cmp_pos, cmp_w1, cmp_w2, w_br_dsa, w_br_nsa, w_br_mla,
           w_o, norm2_g, w_ff_gate, w_ff_up, w_ff_down, final_norm_g):
    bsz, t_len, d = x_prompt.shape
    dec_b, dec_t, _ = x_sample.shape
    depth = w_in.shape[0]
    past = page_table.shape[1] * PAGE_SIZE
    assert dec_t == 1 and t_len % KV_CHUNK == 0 and state_nsa_win.shape[2] == min(WINDOW, past)
    cache_dsa = jnp.swapaxes(cache_dsa, 2, 3)
    cache_mla = jnp.swapaxes(cache_mla, 2, 3)
    pos_p = jnp.arange(t_len, dtype=jnp.int32)
    pos_s = jnp.full((dec_b,), past, dtype=jnp.int32)
    tabs_p = (_rope_tables(pos_p, HD), _rope_tables(pos_p, D_ROPE))
    tabs_s = (_rope_tables(pos_s, HD), _rope_tables(pos_s, D_ROPE))
    gf = final_norm_g[None, :]
    xp = x_prompt.reshape(bsz * t_len, d)
    xs = x_sample.reshape(dec_b, d)
    outs = [[] for _ in range(8)]
    yp = ys = None
    for l in range(depth):
        lw = dict(norm1_g=norm1_g[l], w_in=w_in[l], mla_q_norm_g=mla_q_norm_g[l], w_uq=w_uq[l],
                  mla_kv_norm_g=mla_kv_norm_g[l], w_uk=w_uk[l], w_uv=w_uv[l], cmp_pos=cmp_pos[l],
                  cmp_w1=cmp_w1[l], cmp_w2=cmp_w2[l], w_br_dsa=w_br_dsa[l], w_br_nsa=w_br_nsa[l],
                  w_br_mla=w_br_mla[l], w_o=w_o[l], norm2_g=norm2_g[l], w_ff_gate=w_ff_gate[l],
                  w_ff_up=w_ff_up[l], w_ff_down=w_ff_down[l])
        pw = _prep_layer(lw)
        xp, yp, r_dsa, r_nsa, r_mla, r_win = _layer_prompt(xp, pw, gf, *tabs_p, bsz, t_len, l == depth - 1)
        wlen = min(WINDOW, t_len)
        outs[0].append(r_dsa.reshape(bsz, t_len, DSA_ROW))
        outs[2].append(r_nsa.reshape(bsz, t_len, NSA_ROW))
        outs[4].append(r_mla.reshape(bsz, t_len, MLA_ROW))
        outs[6].append(r_win.reshape(bsz, t_len, WIN_ROW)[:, t_len - wlen:])
        xs, ys, r_dsa, r_nsa, r_mla, r_win = _layer_sample(xs, pw, gf, *tabs_s, l, cache_dsa, cache_nsa, cache_mla,
                                                           state_nsa_win[l], page_table, l == depth - 1)
        outs[1].append(r_dsa.reshape(dec_b, 1, DSA_ROW))
        outs[3].append(r_nsa.reshape(dec_b, 1, NSA_ROW))
        outs[5].append(r_mla.reshape(dec_b, 1, MLA_ROW))
        outs[7].append(r_win)
    return (yp.reshape(bsz, t_len, d), ys.reshape(dec_b, 1, d), *(jnp.stack(o) for o in outs))
```
